```python
import math
import jax, jax.numpy as jnp
from jax import lax
import numpy as np

D_MODEL = 1024
BATCH = 8
SEQ = 4096
DEPTH = 4

CHUNK = 64
N_MIXERS = 3
N_A = len(range(0, DEPTH, N_MIXERS))
N_B = len(range(1, DEPTH, N_MIXERS))
N_C = len(range(2, DEPTH, N_MIXERS))

DN_ALPHA = (2.0 * DEPTH) ** 0.25
DN_BETA = (8.0 * DEPTH) ** -0.25
LN_EPS = 1e-5
RMS_EPS = 1e-6

SSM_EXPAND = 2
SSM_D_INNER = SSM_EXPAND * D_MODEL
SSM_HEAD_DIM = 64
SSM_HEADS = SSM_D_INNER // SSM_HEAD_DIM
SSM_GROUPS = 8
SSM_HPG = SSM_HEADS // SSM_GROUPS
SSM_STATE = 128
SSM_CONV = 4
SSM_CHUNK = CHUNK
SSM_CONV_DIM = SSM_D_INNER + 2 * SSM_GROUPS * SSM_STATE
SSM_IN_DIM = 2 * SSM_D_INNER + 2 * SSM_GROUPS * SSM_STATE + SSM_HEADS

SG_BLOCK = 128
SG_WIDTH = 2 * D_MODEL
SG_GROUPS = 8
SG_GROUP_DIM = SG_WIDTH // SG_GROUPS

MLA_HEADS = 16
MLA_Q_RANK = 384
MLA_KV_RANK = 256
MLA_NOPE = 64
MLA_ROPE = 32
MLA_V = 64
MLA_IN_DIM = MLA_Q_RANK + MLA_KV_RANK + MLA_ROPE
ROPE_THETA = 10000.0
Q_BLOCK = 128

FFN_HIDDEN = 2816
FFN_CONV = 3

kernel_name = "hybrid_ssd_gmlp_mla_deepnorm_trunk"


def layer_norm(x, g, b):
    xf = x.astype(jnp.float32)
    mu = jnp.mean(xf, -1, keepdims=True)
    var = jnp.mean(jnp.square(xf - mu), -1, keepdims=True)
    return ((xf - mu) * lax.rsqrt(var + LN_EPS) * g + b).astype(x.dtype)


def rms_norm(x, g):
    xf = x.astype(jnp.float32)
    return (xf * lax.rsqrt(jnp.mean(xf * xf, -1, keepdims=True) + RMS_EPS) * g).astype(x.dtype)


def causal_dwconv(x, w, b):
    K, C = w.shape
    y = lax.conv_general_dilated(
        x, w[:, None, :].astype(x.dtype), window_strides=(1,), padding=[(K - 1, 0)],
        dimension_numbers=('NWC', 'WIO', 'NWC'), feature_group_count=C)
    return y + b


def ssd_scan(x, dt, A, Bm, Cm):
    Bsz, L, G, R, P = x.shape
    N = Bm.shape[-1]
    Q = SSM_CHUNK
    nc = L // Q
    x = x.reshape(Bsz, nc, Q, G, R, P)
    dt = dt.reshape(Bsz, nc, Q, G, R)
    Bm = Bm.reshape(Bsz, nc, Q, G, N)
    Cm = Cm.reshape(Bsz, nc, Q, G, N)
    Acum = jnp.cumsum(dt * A, axis=2)
    seg = Acum[:, :, :, None] - Acum[:, :, None, :]
    tri = jnp.tril(jnp.ones((Q, Q), dtype=bool))
    Lmat = jnp.exp(jnp.where(tri[:, :, None, None], seg, -jnp.inf))
    CB = jnp.einsum('bcign,bcjgn->bcijg', Cm, Bm)
    y_diag = jnp.einsum('bcijg,bcijgr,bcjgr,bcjgrp->bcigrp', CB, Lmat, dt, x)
    decay = jnp.exp(Acum[:, :, -1:] - Acum)
    states = jnp.einsum('bcjgn,bcjgr,bcjgrp->bcgrpn', Bm, decay * dt, x)
    chunk_decay = jnp.exp(Acum[:, :, -1])

    def step(h, inp):
        s, d = inp
        return h * d[..., None, None] + s, h

    h0 = jnp.zeros((Bsz, G, R, P, N), jnp.float32)
    _, prev = lax.scan(step, h0, (jnp.moveaxis(states, 1, 0), jnp.moveaxis(chunk_decay, 1, 0)))
    prev = jnp.moveaxis(prev, 0, 1)
    y_off = jnp.einsum('bcign,bcgrpn,bcigr->bcigrp', Cm, prev, jnp.exp(Acum))
    return (y_diag + y_off).reshape(Bsz, L, G, R, P)


def mamba2_mixer(x, w_in, conv_w, conv_b, dt_bias, a_log, d_skip, norm_g, w_out):
    Bsz, L, _ = x.shape
    zxbcdt = x @ w_in
    z, xbc, dt = jnp.split(zxbcdt, [SSM_D_INNER, SSM_D_INNER + SSM_CONV_DIM], axis=-1)
    xbc = jax.nn.silu(causal_dwconv(xbc, conv_w, conv_b))
    xs, Bm, Cm = jnp.split(xbc, [SSM_D_INNER, SSM_D_INNER + SSM_GROUPS * SSM_STATE], axis=-1)
    xs = xs.reshape(Bsz, L, SSM_GROUPS, SSM_HPG, SSM_HEAD_DIM).astype(jnp.float32)
    Bm = Bm.reshape(Bsz, L, SSM_GROUPS, SSM_STATE).astype(jnp.float32)
    Cm = Cm.reshape(Bsz, L, SSM_GROUPS, SSM_STATE).astype(jnp.float32)
    dt = jax.nn.softplus(dt.astype(jnp.float32) + dt_bias.astype(jnp.float32))
    dt = dt.reshape(Bsz, L, SSM_GROUPS, SSM_HPG)
    A = -jnp.exp(a_log.astype(jnp.float32)).reshape(SSM_GROUPS, SSM_HPG)
    y = ssd_scan(xs, dt, A, Bm, Cm)
    y = y + d_skip.astype(jnp.float32).reshape(SSM_GROUPS, SSM_HPG)[:, :, None] * xs
    y = y.reshape(Bsz, L, SSM_D_INNER) * jax.nn.silu(z.astype(jnp.float32))
    y = rms_norm(y.reshape(Bsz, L, SSM_GROUPS, -1),
                 norm_g.astype(jnp.float32).reshape(SSM_GROUPS, -1)).reshape(Bsz, L, SSM_D_INNER)
    return y.astype(x.dtype) @ w_out


def spatial_gating_mixer(x, w_in, b_in, ln_g, ln_b, w_s, b_s, w_out):
    Bsz, L, _ = x.shape
    h = jax.nn.gelu(x @ w_in + b_in)
    u, v = jnp.split(h, 2, axis=-1)
    v = layer_norm(v, ln_g, ln_b)
    nb = L // SG_BLOCK
    v = v.reshape(Bsz, nb, SG_BLOCK, SG_GROUPS, SG_GROUP_DIM)
    causal = jnp.tril(jnp.ones((SG_BLOCK, SG_BLOCK), dtype=bool))
    ws = jnp.where(causal, w_s, 0)
    v = jnp.einsum('gts,bnsgc->bntgc', ws, v) + b_s.T[:, :, None]
    return (u * v.reshape(Bsz, L, SG_WIDTH)) @ w_out


def rope_cos_sin(positions, dtype):
    half = MLA_ROPE // 2
    inv = ROPE_THETA ** (-(jnp.arange(half, dtype=jnp.float32) * 2.0 / MLA_ROPE))
    ang = positions.astype(jnp.float32)[..., None] * inv
    return jnp.cos(ang).astype(dtype), jnp.sin(ang).astype(dtype)


def apply_rope(t, cos, sin):
    half = MLA_ROPE // 2
    t1, t2 = t[..., :half], t[..., half:]
    return jnp.concatenate([t1 * cos - t2 * sin, t2 * cos + t1 * sin], axis=-1)


def mla_mixer(x, positions, w_in, q_norm_g, w_q_b, kv_norm_g, w_kv_b, w_out):
    Bsz, L, _ = x.shape
    H = MLA_HEADS
    q_lat, kv_lat, k_rope = jnp.split(x @ w_in, [MLA_Q_RANK, MLA_Q_RANK + MLA_KV_RANK], axis=-1)
    q = (rms_norm(q_lat, q_norm_g) @ w_q_b).reshape(Bsz, L, H, MLA_NOPE + MLA_ROPE)
    q_nope, q_rope = jnp.split(q, [MLA_NOPE], axis=-1)
    kv = (rms_norm(kv_lat, kv_norm_g) @ w_kv_b).reshape(Bsz, L, H, MLA_NOPE + MLA_V)
    k_nope, v = jnp.split(kv, [MLA_NOPE], axis=-1)
    cos, sin = rope_cos_sin(positions, x.dtype)
    q_rope = apply_rope(q_rope, cos[:, :, None], sin[:, :, None])
    k_rope = apply_rope(k_rope, cos, sin)
    scale = (MLA_NOPE + MLA_ROPE) ** -0.5
    nq = L // Q_BLOCK
    key_chunk = jnp.arange(L) // CHUNK
    qn_blocks = jnp.moveaxis(q_nope.reshape(Bsz, nq, Q_BLOCK, H, MLA_NOPE), 1, 0)
    qr_blocks = jnp.moveaxis(q_rope.reshape(Bsz, nq, Q_BLOCK, H, MLA_ROPE), 1, 0)

    def attend(args):
        qn, qr, qi = args
        s = (jnp.einsum('bqhd,bkhd->bhqk', qn, k_nope)
             + jnp.einsum('bqhd,bkd->bhqk', qr, k_rope)).astype(jnp.float32) * scale
        q_chunk = (qi * Q_BLOCK + jnp.arange(Q_BLOCK)) // CHUNK
        mask = key_chunk[None, :] <= q_chunk[:, None]
        p = jax.nn.softmax(jnp.where(mask, s, -jnp.inf), axis=-1).astype(v.dtype)
        return jnp.einsum('bhqk,bkhd->bqhd', p, v)

    o = lax.map(attend, (qn_blocks, qr_blocks, jnp.arange(nq)))
    o = jnp.moveaxis(o, 0, 1).reshape(Bsz, L, H * MLA_V)
    return o @ w_out


def conv_ffn(x, w_in, conv_w, conv_b, w_out):
    h = causal_dwconv(x @ w_in, conv_w, conv_b)
    g, u = jnp.split(h, 2, axis=-1)
    return (jax.nn.silu(g) * u) @ w_out


def setup_inputs(seed: int = 0) -> dict:
    key = jax.random.key(seed)
    ks = iter(jax.random.split(key, 40))
    f32 = jnp.float32

    def nrm(shape, scale):
        return jax.random.normal(next(ks), shape, f32) * scale

    def gain(shape):
        return 1.0 + nrm(shape, 0.02)

    x = jax.random.normal(next(ks), (BATCH, SEQ, D_MODEL), f32)
    offsets = jax.random.randint(next(ks), (BATCH, 1), 0, 4096, dtype=jnp.int32)
    positions = offsets + jnp.arange(SEQ, dtype=jnp.int32)[None, :]

    ssm_w_in = nrm((N_A, D_MODEL, SSM_IN_DIM), D_MODEL ** -0.5)
    ssm_conv_w = nrm((N_A, SSM_CONV, SSM_CONV_DIM), SSM_CONV ** -0.5)
    ssm_conv_b = nrm((N_A, SSM_CONV_DIM), 0.02)
    dt0 = jnp.exp(jax.random.uniform(next(ks), (N_A, SSM_HEADS), f32, math.log(1e-3), math.log(1e-1)))
    ssm_dt_bias = dt0 + jnp.log(-jnp.expm1(-dt0))
    ssm_a_log = jnp.log(jax.random.uniform(next(ks), (N_A, SSM_HEADS), f32, 1.0, 16.0))
    ssm_d = gain((N_A, SSM_HEADS))
    ssm_norm_g = gain((N_A, SSM_D_INNER))
    ssm_w_out = nrm((N_A, SSM_D_INNER, D_MODEL), SSM_D_INNER ** -0.5 * DN_BETA)

    sg_w_in = nrm((N_B, D_MODEL, 2 * SG_WIDTH), D_MODEL ** -0.5)
    sg_b_in = nrm((N_B, 2 * SG_WIDTH), 0.02)
    sg_ln_g = gain((N_B, SG_WIDTH))
    sg_ln_b = nrm((N_B, SG_WIDTH), 0.02)
    sg_w_s = nrm((N_B, SG_GROUPS, SG_BLOCK, SG_BLOCK), SG_BLOCK ** -0.5)
    sg_b_s = gain((N_B, SG_GROUPS, SG_BLOCK))
    sg_w_out = nrm((N_B, SG_WIDTH, D_MODEL), SG_WIDTH ** -0.5 * DN_BETA)

    mla_w_in = nrm((N_C, D_MODEL, MLA_IN_DIM), D_MODEL ** -0.5)
    mla_q_norm_g = gain((N_C, MLA_Q_RANK))
    mla_w_q_b = nrm((N_C, MLA_Q_RANK, MLA_HEADS * (MLA_NOPE + MLA_ROPE)), MLA_Q_RANK ** -0.5)
    mla_kv_norm_g = gain((N_C, MLA_KV_RANK))
    mla_w_kv_b = nrm((N_C, MLA_KV_RANK, MLA_HEADS * (MLA_NOPE + MLA_V)), MLA_KV_RANK ** -0.5)
    mla_w_out = nrm((N_C, MLA_HEADS * MLA_V, D_MODEL), (MLA_HEADS * MLA_V) ** -0.5 * DN_BETA)

    ffn_w_in = nrm((DEPTH, D_MODEL, 2 * FFN_HIDDEN), D_MODEL ** -0.5)
    ffn_conv_w = nrm((DEPTH, FFN_CONV, 2 * FFN_HIDDEN), FFN_CONV ** -0.5)
    ffn_conv_b = nrm((DEPTH, 2 * FFN_HIDDEN), 0.02)
    ffn_w_out = nrm((DEPTH, FFN_HIDDEN, D_MODEL), FFN_HIDDEN ** -0.5 * DN_BETA)

    ln_g = gain((DEPTH, 2, D_MODEL))
    ln_b = nrm((DEPTH, 2, D_MODEL), 0.02)

    return {
        "x": x, "positions": positions,
        "ssm_w_in": ssm_w_in, "ssm_conv_w": ssm_conv_w, "ssm_conv_b": ssm_conv_b,
        "ssm_dt_bias": ssm_dt_bias, "ssm_a_log": ssm_a_log, "ssm_d": ssm_d,
        "ssm_norm_g": ssm_norm_g, "ssm_w_out": ssm_w_out,
        "sg_w_in": sg_w_in, "sg_b_in": sg_b_in, "sg_ln_g": sg_ln_g, "sg_ln_b": sg_ln_b,
        "sg_w_s": sg_w_s, "sg_b_s": sg_b_s, "sg_w_out": sg_w_out,
        "mla_w_in": mla_w_in, "mla_q_norm_g": mla_q_norm_g, "mla_w_q_b": mla_w_q_b,
        "mla_kv_norm_g": mla_kv_norm_g, "mla_w_kv_b": mla_w_kv_b, "mla_w_out": mla_w_out,
        "ffn_w_in": ffn_w_in, "ffn_conv_w": ffn_conv_w, "ffn_conv_b": ffn_conv_b,
        "ffn_w_out": ffn_w_out,
        "ln_g": ln_g, "ln_b": ln_b,
    }


def reference(x, positions,
              ssm_w_in, ssm_conv_w, ssm_conv_b, ssm_dt_bias, ssm_a_log, ssm_d, ssm_norm_g, ssm_w_out,
              sg_w_in, sg_b_in, sg_ln_g, sg_ln_b, sg_w_s, sg_b_s, sg_w_out,
              mla_w_in, mla_q_norm_g, mla_w_q_b, mla_kv_norm_g, mla_w_kv_b, mla_w_out,
              ffn_w_in, ffn_conv_w, ffn_conv_b, ffn_w_out,
              ln_g, ln_b):
    for i in range(DEPTH):
        m, j = i % N_MIXERS, i // N_MIXERS
        if m == 0:
            y = mamba2_mixer(x, ssm_w_in[j], ssm_conv_w[j], ssm_conv_b[j], ssm_dt_bias[j],
                             ssm_a_log[j], ssm_d[j], ssm_norm_g[j], ssm_w_out[j])
        elif m == 1:
            y = spatial_gating_mixer(x, sg_w_in[j], sg_b_in[j], sg_ln_g[j], sg_ln_b[j],
                                     sg_w_s[j], sg_b_s[j], sg_w_out[j])
        else:
            y = mla_mixer(x, positions, mla_w_in[j], mla_q_norm_g[j], mla_w_q_b[j],
                          mla_kv_norm_g[j], mla_w_kv_b[j], mla_w_out[j])
        x = layer_norm(DN_ALPHA * x + y, ln_g[i, 0], ln_b[i, 0])
        f = conv_ffn(x, ffn_w_in[i], ffn_conv_w[i], ffn_conv_b[i], ffn_w_out[i])
        x = layer_norm(DN_ALPHA * x + f, ln_g[i, 1], ln_b[i, 1])
    return x
```

```python
import functools
import math

import jax
import jax.numpy as jnp
from jax import lax
from jax.experimental import pallas as pl
from jax.experimental.pallas import tpu as pltpu

D_MODEL = 1024
BATCH = 8
SEQ = 4096
DEPTH = 4
TOKENS = BATCH * SEQ

CHUNK = 64
N_MIXERS = 3

DN_ALPHA = (2.0 * DEPTH) ** 0.25
LN_EPS = 1e-5
RMS_EPS = 1e-6

SSM_D_INNER = 2048
SSM_HEAD_DIM = 64
SSM_HEADS = 32
SSM_GROUPS = 8
SSM_HPG = 4
SSM_STATE = 128
SSM_CONV = 4
SSM_CONV_DIM = 4096
SSM_BC = SSM_GROUPS * SSM_STATE

SG_BLOCK = 128
SG_WIDTH = 2048
SG_GROUPS = 8
SG_GROUP_DIM = 256

MLA_HEADS = 16
MLA_Q_RANK = 384
MLA_KV_RANK = 256
MLA_NOPE = 64
MLA_ROPE = 32
MLA_HALF = 16
MLA_V = 64
MLA_QK = MLA_NOPE + MLA_ROPE
ROPE_THETA = 10000.0

FFN_HIDDEN = 2816
FFN_CONV = 3

LANES = 128
SUBLANES = 8
HALO = SUBLANES
VMEM_LIMIT = 56 * 1024 * 1024

BF16 = jnp.bfloat16
F32 = jnp.float32
NEG_BIG = -1e30


def _dot(a, b):
    return jnp.dot(a, b, preferred_element_type=F32)


def _dot_nt(a, b):
    return lax.dot_general(a, b, (((1,), (1,)), ((), ())), preferred_element_type=F32)


def _layer_norm(z, g, b):
    mu = jnp.mean(z, axis=-1, keepdims=True)
    zc = z - mu
    var = jnp.mean(zc * zc, axis=-1, keepdims=True)
    return zc * lax.rsqrt(var + LN_EPS) * g + b


def _silu(x):
    return x * (1.0 / (1.0 + jnp.exp(-x)))


def _const_spec(shape):
    nd = len(shape)
    return pl.BlockSpec(shape, lambda *_: (0,) * nd, pipeline_mode=pl.Buffered(1))


def _params(sem):
    return pltpu.CompilerParams(dimension_semantics=sem, vmem_limit_bytes=VMEM_LIMIT)


FFN_TM = 512
FFN_CHUNKS = ((0, 768), (768, 768), (1536, 768), (2304, 512))


def _ffn_kernel(x_ref, win_ref, cw_ref, cb_ref, wout_ref, g_ref, b_ref, o_ref, hs_ref, acc_ref):
    tm = FFN_TM
    i = pl.program_id(0)
    first = (i % (SEQ // tm)) == 0

    @pl.when(first)
    def _():
        hs_ref[:, 0:HALO, :] = jnp.zeros((2, HALO, FFN_HIDDEN), F32)

    @pl.when(jnp.logical_not(first))
    def _():
        hs_ref[:, 0:HALO, :] = hs_ref[:, tm:tm + HALO, :]

    x = x_ref[...]
    xb = x.astype(BF16)
    for ci, (c0, cw) in enumerate(FFN_CHUNKS):
        conv = []
        for half in range(2):
            off = half * FFN_HIDDEN + c0
            hs_ref[half, HALO:HALO + tm, c0:c0 + cw] = _dot(xb, win_ref[:, off:off + cw])
            c = cb_ref[:, off:off + cw]
            for k in range(FFN_CONV):
                r0 = HALO - (FFN_CONV - 1) + k
                c = c + cw_ref[k:k + 1, off:off + cw] * hs_ref[half, r0:r0 + tm, c0:c0 + cw]
            conv.append(c)
        act = (_silu(conv[0]) * conv[1]).astype(BF16)
        part = _dot(act, wout_ref[c0:c0 + cw, :])
        if ci == 0:
            acc_ref[...] = part
        else:
            acc_ref[...] += part
    z = DN_ALPHA * x + acc_ref[...]
    o_ref[...] = _layer_norm(z, g_ref[...], b_ref[...])


def _ffn(x, w_in, conv_w, conv_b, w_out, g, b):
    tm = FFN_TM
    return pl.pallas_call(
        _ffn_kernel,
        grid=(TOKENS // tm,),
        in_specs=[
            pl.BlockSpec((tm, D_MODEL), lambda i: (i, 0)),
            _const_spec((D_MODEL, 2 * FFN_HIDDEN)),
            _const_spec((FFN_CONV, 2 * FFN_HIDDEN)),
            _const_spec((1, 2 * FFN_HIDDEN)),
            _const_spec((FFN_HIDDEN, D_MODEL)),
            _const_spec((1, D_MODEL)),
            _const_spec((1, D_MODEL)),
        ],
        out_specs=pl.BlockSpec((tm, D_MODEL), lambda i: (i, 0)),
        out_shape=jax.ShapeDtypeStruct((TOKENS, D_MODEL), F32),
        scratch_shapes=[
            pltpu.VMEM((2, tm + HALO, FFN_HIDDEN), F32),
            pltpu.VMEM((tm, D_MODEL), F32),
        ],
        compiler_params=_params(("arbitrary",)),
        name="conv_ffn",
    )(x, w_in, conv_w, conv_b, w_out, g, b)


SG_TM = 512


def _sg_kernel(x_ref, win_ref, bin_ref, lng_ref, lnb_ref, ws_ref, bs_ref, wout_ref, g_ref, b_ref,
               o_ref, u_ref, v_ref, gated_ref):
    tm = SG_TM
    x = x_ref[...]
    xb = x.astype(BF16)
    u_ref[...] = jax.nn.gelu(_dot(xb, win_ref[:, 0:SG_WIDTH]) + bin_ref[:, 0:SG_WIDTH])
    v = jax.nn.gelu(_dot(xb, win_ref[:, SG_WIDTH:2 * SG_WIDTH]) + bin_ref[:, SG_WIDTH:2 * SG_WIDTH])
    v_ref[...] = _layer_norm(v, lng_ref[...], lnb_ref[...]).astype(BF16)

    row = lax.broadcasted_iota(jnp.int32, (SG_BLOCK, SG_BLOCK), 0)
    col = lax.broadcasted_iota(jnp.int32, (SG_BLOCK, SG_BLOCK), 1)
    causal = col <= row
    for grp in range(SG_GROUPS):
        c0 = grp * SG_GROUP_DIM
        ws = jnp.where(causal, ws_ref[grp], 0.0).astype(BF16)
        bias = bs_ref[:, c0:c0 + SG_GROUP_DIM]
        for blk in range(tm // SG_BLOCK):
            r0 = blk * SG_BLOCK
            mixed = _dot(ws, v_ref[r0:r0 + SG_BLOCK, c0:c0 + SG_GROUP_DIM]) + bias
            gated_ref[r0:r0 + SG_BLOCK, c0:c0 + SG_GROUP_DIM] = (
                u_ref[r0:r0 + SG_BLOCK, c0:c0 + SG_GROUP_DIM] * mixed).astype(BF16)
    z = DN_ALPHA * x + _dot(gated_ref[...], wout_ref[...])
    o_ref[...] = _layer_norm(z, g_ref[...], b_ref[...])


def _spatial_gating(x, w_in, b_in, ln_g, ln_b, w_s, b_s_wide, w_out, g, b):
    tm = SG_TM
    return pl.pallas_call(
        _sg_kernel,
        grid=(TOKENS // tm,),
        in_specs=[
            pl.BlockSpec((tm, D_MODEL), lambda i: (i, 0)),
            _const_spec((D_MODEL, 2 * SG_WIDTH)),
            _const_spec((1, 2 * SG_WIDTH)),
            _const_spec((1, SG_WIDTH)),
            _const_spec((1, SG_WIDTH)),
            _const_spec((SG_GROUPS, SG_BLOCK, SG_BLOCK)),
            _const_spec((SG_BLOCK, SG_WIDTH)),
            _const_spec((SG_WIDTH, D_MODEL)),
            _const_spec((1, D_MODEL)),
            _const_spec((1, D_MODEL)),
        ],
        out_specs=pl.BlockSpec((tm, D_MODEL), lambda i: (i, 0)),
        out_shape=jax.ShapeDtypeStruct((TOKENS, D_MODEL), F32),
        scratch_shapes=[
            pltpu.VMEM((tm, SG_WIDTH), F32),
            pltpu.VMEM((tm, SG_WIDTH), BF16),
            pltpu.VMEM((tm, SG_WIDTH), BF16),
        ],
        compiler_params=_params(("arbitrary",)),
        name="spatial_gating",
    )(x, w_in, b_in, ln_g, ln_b, w_s, b_s_wide, w_out, g, b)


ROPE_TL = 512
MLA_TM = 512
MLA_LAT = MLA_Q_RANK + MLA_KV_RANK + 2 * LANES
MLA_HEAD_PAD = LANES
ATT_TQ = 256
ATT_TK = 256
ATT_HEADS_PER_STEP = 2


def _rope_kernel(pos_ref, inv_ref, cos_t_ref, sin_t_ref, cos_tab_ref, sin_tab_ref):
    ang = inv_ref[...] * pos_ref[0].astype(F32)
    c = jnp.cos(ang)
    s = jnp.sin(ang)
    cos_t_ref[0] = c
    sin_t_ref[0] = s
    zlo = jnp.zeros((MLA_NOPE, ROPE_TL), F32)
    zhi = jnp.zeros((LANES - MLA_QK, ROPE_TL), F32)
    cos_tab_ref[...] = jnp.concatenate([zlo, c, c, zhi], axis=0).T
    sin_tab_ref[...] = jnp.concatenate([zlo, s, s, zhi], axis=0).T


def _rope_tables(positions, inv):
    nl = SEQ // ROPE_TL
    pos3 = positions.reshape(BATCH, 1, SEQ)
    return pl.pallas_call(
        _rope_kernel,
        grid=(BATCH, nl),
        in_specs=[
            pl.BlockSpec((1, 1, ROPE_TL), lambda b, j: (b, 0, j)),
            pl.BlockSpec((MLA_HALF, 1), lambda b, j: (0, 0)),
        ],
        out_specs=[
            pl.BlockSpec((1, MLA_HALF, ROPE_TL), lambda b, j: (b, 0, j)),
            pl.BlockSpec((1, MLA_HALF, ROPE_TL), lambda b, j: (b, 0, j)),
            pl.BlockSpec((ROPE_TL, LANES), lambda b, j: (b * nl + j, 0)),
            pl.BlockSpec((ROPE_TL, LANES), lambda b, j: (b * nl + j, 0)),
        ],
        out_shape=[
            jax.ShapeDtypeStruct((BATCH, MLA_HALF, SEQ), F32),
            jax.ShapeDtypeStruct((BATCH, MLA_HALF, SEQ), F32),
            jax.ShapeDtypeStruct((TOKENS, LANES), F32),
            jax.ShapeDtypeStruct((TOKENS, LANES), F32),
        ],
        compiler_params=_params(("arbitrary", "arbitrary")),
        name="rope_tables",
    )(pos3, inv)


def _rms(x, g):
    return x * lax.rsqrt(jnp.mean(x * x, axis=-1, keepdims=True) + RMS_EPS) * g


def _mla_proj_kernel(x_ref, win_ref, gq_ref, gkv_ref, wqt_ref, wk_ref, wvt_ref,
                     cos_t_ref, sin_t_ref, cos_tab_ref, sin_tab_ref,
                     qt_ref, k_ref, vt_ref):
    xb = x_ref[...].astype(BF16)
    lat = _dot(xb, win_ref[...])
    qn = _rms(lat[:, 0:MLA_Q_RANK], gq_ref[...]).astype(BF16)
    kvn = _rms(lat[:, MLA_Q_RANK:MLA_Q_RANK + MLA_KV_RANK], gkv_ref[...]).astype(BF16)
    s0 = MLA_Q_RANK + MLA_KV_RANK
    kslab = lat[:, s0:s0 + LANES] * cos_tab_ref[...] + lat[:, s0 + LANES:s0 + 2 * LANES] * sin_tab_ref[...]
    kmat = _dot(kvn, wk_ref[...])
    for h in range(MLA_HEADS):
        c0 = h * MLA_HEAD_PAD
        k_ref[:, c0:c0 + MLA_HEAD_PAD] = (kmat[:, c0:c0 + MLA_HEAD_PAD] + kslab).astype(BF16)
    vt_ref[0] = _dot_nt(wvt_ref[...], kvn).astype(BF16)
    qt = _dot_nt(wqt_ref[...], qn)
    cos_t = cos_t_ref[0]
    sin_t = sin_t_ref[0]
    for h in range(MLA_HEADS):
        r0 = h * MLA_HEAD_PAD
        qt_ref[0, r0:r0 + MLA_NOPE, :] = qt[r0:r0 + MLA_NOPE, :].astype(BF16)
        t1 = qt[r0 + MLA_NOPE:r0 + MLA_NOPE + MLA_HALF, :]
        t2 = qt[r0 + MLA_NOPE + MLA_HALF:r0 + MLA_QK, :]
        qt_ref[0, r0 + MLA_NOPE:r0 + MLA_NOPE + MLA_HALF, :] = (t1 * cos_t - t2 * sin_t).astype(BF16)
        qt_ref[0, r0 + MLA_NOPE + MLA_HALF:r0 + MLA_QK, :] = (t2 * cos_t + t1 * sin_t).astype(BF16)
        qt_ref[0, r0 + MLA_QK:r0 + MLA_HEAD_PAD, :] = jnp.zeros((MLA_HEAD_PAD - MLA_QK, MLA_TM), BF16)


def _mla_proj(x, w_in_pad, gq, gkv, wqt, wk, wvt, cos_t, sin_t, cos_tab, sin_tab):
    tm = MLA_TM
    nl = SEQ // tm
    hp = MLA_HEADS * MLA_HEAD_PAD
    return pl.pallas_call(
        _mla_proj_kernel,
        grid=(BATCH, nl),
        in_specs=[
            pl.BlockSpec((tm, D_MODEL), lambda b, j: (b * nl + j, 0)),
            _const_spec((D_MODEL, MLA_LAT)),
            _const_spec((1, MLA_Q_RANK)),
            _const_spec((1, MLA_KV_RANK)),
            _const_spec((hp, MLA_Q_RANK)),
            _const_spec((MLA_KV_RANK, hp)),
            _const_spec((MLA_HEADS * MLA_V, MLA_KV_RANK)),
            pl.BlockSpec((1, MLA_HALF, tm), lambda b, j: (b, 0, j)),
            pl.BlockSpec((1, MLA_HALF, tm), lambda b, j: (b, 0, j)),
            pl.BlockSpec((tm, LANES), lambda b, j: (b * nl + j, 0)),
            pl.BlockSpec((tm, LANES), lambda b, j: (b * nl + j, 0)),
        ],
        out_specs=[
            pl.BlockSpec((1, hp, tm), lambda b, j: (b, 0, j)),
            pl.BlockSpec((tm, hp), lambda b, j: (b * nl + j, 0)),
            pl.BlockSpec((1, MLA_HEADS * MLA_V, tm), lambda b, j: (b, 0, j)),
        ],
        out_shape=[
            jax.ShapeDtypeStruct((BATCH, hp, SEQ), BF16),
            jax.ShapeDtypeStruct((TOKENS, hp), BF16),
            jax.ShapeDtypeStruct((BATCH, MLA_HEADS * MLA_V, SEQ), BF16),
        ],
        compiler_params=_params(("arbitrary", "arbitrary")),
        name="mla_proj",
    )(x, w_in_pad, gq, gkv, wqt, wk, wvt, cos_t, sin_t, cos_tab, sin_tab)


def _attn_kernel(qt_ref, k_ref, vt_ref, o_ref):
    tq, tk = ATT_TQ, ATT_TK
    qi = pl.program_id(2)
    c = (MLA_QK ** -0.5) * math.log2(math.e)
    nh = ATT_HEADS_PER_STEP

    def step(j, carry, masked):
        k0 = pl.multiple_of(j * tk, tk)
        out = []
        for hh in range(nh):
            m, l, acc = carry[hh]
            kt = k_ref[pl.ds(k0, tk), hh * MLA_HEAD_PAD:(hh + 1) * MLA_HEAD_PAD]
            st = _dot(kt, qt_ref[0, hh * MLA_HEAD_PAD:(hh + 1) * MLA_HEAD_PAD, :])
            if masked:
                krow = lax.broadcasted_iota(jnp.int32, (tk, tq), 0) // CHUNK
                qcol = lax.broadcasted_iota(jnp.int32, (tk, tq), 1) // CHUNK
                st = jnp.where(krow <= qcol, st, NEG_BIG)
            m_new = jnp.maximum(m, jnp.max(st, axis=0, keepdims=True))
            alpha = jnp.exp2((m - m_new) * c)
            p = jnp.exp2((st - m_new) * c)
            l = alpha * l + jnp.sum(p, axis=0, keepdims=True)
            vt = vt_ref[0, hh * MLA_V:(hh + 1) * MLA_V, pl.ds(k0, tk)]
            acc = alpha * acc + _dot(vt, p.astype(BF16))
            out.append((m_new, l, acc))
        return tuple(out)

    init = tuple((jnp.full((1, tq), NEG_BIG, F32), jnp.zeros((1, tq), F32), jnp.zeros((MLA_V, tq), F32))
                 for _ in range(nh))
    carry = lax.fori_loop(0, qi, lambda j, cr: step(j, cr, False), init)
    carry = step(qi, carry, True)
    outs = [(acc / l).T for (_, l, acc) in carry]
    o_ref[...] = jnp.concatenate(outs, axis=1).astype(BF16)


def _attention(qt, k, vt):
    assert ATT_TQ == ATT_TK and ATT_TQ % CHUNK == 0
    nh = ATT_HEADS_PER_STEP
    nq = SEQ // ATT_TQ
    return pl.pallas_call(
        _attn_kernel,
        grid=(BATCH, MLA_HEADS // nh, nq),
        in_specs=[
            pl.BlockSpec((1, nh * MLA_HEAD_PAD, ATT_TQ), lambda b, h, q: (b, h, q)),
            pl.BlockSpec((SEQ, nh * MLA_HEAD_PAD), lambda b, h, q: (b, h)),
            pl.BlockSpec((1, nh * MLA_V, SEQ), lambda b, h, q: (b, h, 0)),
        ],
        out_specs=pl.BlockSpec((ATT_TQ, nh * MLA_V), lambda b, h, q: (b * nq + q, h)),
        out_shape=jax.ShapeDtypeStruct((TOKENS, MLA_HEADS * MLA_V), BF16),
        compiler_params=_params(("arbitrary", "arbitrary", "arbitrary")),
        name="mla_attention",
    )(qt, k, vt)


PROJ_TM = 512


def _proj_ln_kernel(x_ref, a_ref, w_ref, g_ref, b_ref, o_ref):
    z = DN_ALPHA * x_ref[...] + _dot(a_ref[...], w_ref[...])
    o_ref[...] = _layer_norm(z, g_ref[...], b_ref[...])


def _proj_ln(x, a, w, g, b):
    tm = PROJ_TM
    kdim = a.shape[1]
    return pl.pallas_call(
        _proj_ln_kernel,
        grid=(TOKENS // tm,),
        in_specs=[
            pl.BlockSpec((tm, D_MODEL), lambda i: (i, 0)),
            pl.BlockSpec((tm, kdim), lambda i: (i, 0)),
            _const_spec((kdim, D_MODEL)),
            _const_spec((1, D_MODEL)),
            _const_spec((1, D_MODEL)),
        ],
        out_specs=pl.BlockSpec((tm, D_MODEL), lambda i: (i, 0)),
        out_shape=jax.ShapeDtypeStruct((TOKENS, D_MODEL), F32),
        compiler_params=_params(("arbitrary",)),
        name="proj_ln",
    )(x, a, w, g, b)


SSM_TM = 512
SSD_Q = 128


def _ssm_in_kernel(x_ref, wz_ref, wxbc_ref, wdt_ref, cw_ref, cb_ref, dtb_ref,
                   z_ref, xs_ref, b_ref, bt_ref, c_ref, dt_ref, dtt_ref, hs_ref):
    tm = SSM_TM
    first = pl.program_id(1) == 0

    @pl.when(first)
    def _():
        hs_ref[0:HALO, :] = jnp.zeros((HALO, SSM_CONV_DIM), F32)

    @pl.when(jnp.logical_not(first))
    def _():
        hs_ref[0:HALO, :] = hs_ref[tm:tm + HALO, :]

    xb = x_ref[...].astype(BF16)
    z_ref[...] = _dot(xb, wz_ref[...]).astype(BF16)
    hs_ref[HALO:HALO + tm, :] = _dot(xb, wxbc_ref[...])
    dt = jax.nn.softplus(_dot(xb, wdt_ref[...]) + dtb_ref[...])
    dt_ref[...] = dt
    dtt_ref[0] = dt.T[0:SSM_HEADS, :]

    def conv(c0, cw):
        acc = cb_ref[:, c0:c0 + cw]
        for k in range(SSM_CONV):
            r0 = HALO - (SSM_CONV - 1) + k
            acc = acc + cw_ref[k:k + 1, c0:c0 + cw] * hs_ref[r0:r0 + tm, c0:c0 + cw]
        return _silu(acc)

    half = SSM_D_INNER // 2
    xs_ref[:, 0:half] = conv(0, half).astype(BF16)
    xs_ref[:, half:SSM_D_INNER] = conv(half, half).astype(BF16)
    bm = conv(SSM_D_INNER, SSM_BC)
    b_ref[...] = bm.astype(BF16)
    bt_ref[0] = bm.T.astype(BF16)
    c_ref[...] = conv(SSM_D_INNER + SSM_BC, SSM_BC).astype(BF16)


def _ssm_in(x, w_z, w_xbc, w_dt, conv_w, conv_b, dt_bias):
    tm = SSM_TM
    nl = SEQ // tm
    tok = lambda n: pl.BlockSpec((tm, n), lambda b, j: (b * nl + j, 0))
    return pl.pallas_call(
        _ssm_in_kernel,
        grid=(BATCH, nl),
        in_specs=[
            tok(D_MODEL),
            _const_spec((D_MODEL, SSM_D_INNER)),
            _const_spec((D_MODEL, SSM_CONV_DIM)),
            _const_spec((D_MODEL, LANES)),
            _const_spec((SSM_CONV, SSM_CONV_DIM)),
            _const_spec((1, SSM_CONV_DIM)),
            _const_spec((1, LANES)),
        ],
        out_specs=[
            tok(SSM_D_INNER),
            tok(SSM_D_INNER),
            tok(SSM_BC),
            pl.BlockSpec((1, SSM_BC, tm), lambda b, j: (b, 0, j)),
            tok(SSM_BC),
            tok(LANES),
            pl.BlockSpec((1, SSM_HEADS, tm), lambda b, j: (b, 0, j)),
        ],
        out_shape=[
            jax.ShapeDtypeStruct((TOKENS, SSM_D_INNER), BF16),
            jax.ShapeDtypeStruct((TOKENS, SSM_D_INNER), BF16),
            jax.ShapeDtypeStruct((TOKENS, SSM_BC), BF16),
            jax.ShapeDtypeStruct((BATCH, SSM_BC, SEQ), BF16),
            jax.ShapeDtypeStruct((TOKENS, SSM_BC), BF16),
            jax.ShapeDtypeStruct((TOKENS, LANES), F32),
            jax.ShapeDtypeStruct((BATCH, SSM_HEADS, SEQ), F32),
        ],
        scratch_shapes=[pltpu.VMEM((tm + HALO, SSM_CONV_DIM), F32)],
        compiler_params=_params(("arbitrary", "arbitrary")),
        name="ssm_in",
    )(x, w_z, w_xbc, w_dt, conv_w, conv_b, dt_bias)


def _ssd_kernel(x_ref, z_ref, xs_ref, b_ref, bt_ref, c_ref, dt_ref, dtt_ref,
                alog_row_ref, alog_col_ref, dskip_ref, ng_ref, wout_ref, g_ref, bb_ref,
                o_ref, state_ref, y_ref):
    q = SSD_Q
    hi = lax.Precision.HIGHEST

    @pl.when(pl.program_id(1) == 0)
    def _():
        state_ref[...] = jnp.zeros(state_ref.shape, F32)

    row = lax.broadcasted_iota(jnp.int32, (q, q), 0)
    col = lax.broadcasted_iota(jnp.int32, (q, q), 1)
    lower = col <= row
    tri = lower.astype(F32)
    a = dt_ref[...] * (-jnp.exp(alog_row_ref[...]))
    a_t = dtt_ref[0] * (-jnp.exp(alog_col_ref[...]))
    acum = jnp.dot(tri, a, precision=hi, preferred_element_type=F32)
    acum_t = lax.dot_general(a_t, tri, (((1,), (1,)), ((), ())), precision=hi,
                             preferred_element_type=F32)
    e_col = jnp.exp(acum)
    w_t = jnp.exp(acum_t[:, q - 1:q] - acum_t) * dtt_ref[0]
    cd = jnp.exp(acum_t[:, q - 1:q])
    dt_t = dtt_ref[0]

    for grp in range(SSM_GROUPS):
        n0 = grp * SSM_STATE
        cg = c_ref[:, n0:n0 + SSM_STATE]
        cb = _dot_nt(cg, b_ref[:, n0:n0 + SSM_STATE])
        cgf = cg.astype(F32)
        btf = bt_ref[0, n0:n0 + SSM_STATE, :].astype(F32)
        for r in range(SSM_HPG):
            h = grp * SSM_HPG + r
            p0 = h * SSM_HEAD_DIM
            xh = xs_ref[:, p0:p0 + SSM_HEAD_DIM]
            seg = acum[:, h:h + 1] - acum_t[h:h + 1, :]
            lmat = jnp.exp(jnp.where(lower, seg, NEG_BIG))
            mh = (cb * lmat * dt_t[h:h + 1, :]).astype(BF16)
            ec = (cgf * e_col[:, h:h + 1]).astype(BF16)
            s_old = state_ref[h]
            lhs = jnp.concatenate([mh, ec], axis=1)
            rhs = jnp.concatenate([xh, s_old.astype(BF16)], axis=0)
            yh = _dot(lhs, rhs) + dskip_ref[:, p0:p0 + SSM_HEAD_DIM] * xh.astype(F32)
            y_ref[:, p0:p0 + SSM_HEAD_DIM] = yh
            btw = (btf * w_t[h:h + 1, :]).astype(BF16)
            state_ref[h] = s_old * cd[h:h + 1, :] + _dot(btw, xh)

    zf = z_ref[...].astype(F32)
    y = y_ref[...] * _silu(zf)
    for grp in range(SSM_GROUPS):
        c0 = grp * SSM_HPG * SSM_HEAD_DIM
        cw = SSM_HPG * SSM_HEAD_DIM
        yg = y[:, c0:c0 + cw]
        y_ref[:, c0:c0 + cw] = _rms(yg, ng_ref[:, c0:c0 + cw])
    out = _dot(y_ref[...].astype(BF16), wout_ref[...])
    zres = DN_ALPHA * x_ref[...] + out
    o_ref[...] = _layer_norm(zres, g_ref[...], bb_ref[...])


def _ssd(x, z, xs, bm, bt, cm, dt, dtt, alog_row, alog_col, dskip, ng, w_out, g, b):
    q = SSD_Q
    nl = SEQ // q
    tok = lambda n: pl.BlockSpec((q, n), lambda bi, j: (bi * nl + j, 0))
    return pl.pallas_call(
        _ssd_kernel,
        grid=(BATCH, nl),
        in_specs=[
            tok(D_MODEL),
            tok(SSM_D_INNER),
            tok(SSM_D_INNER),
            tok(SSM_BC),
            pl.BlockSpec((1, SSM_BC, q), lambda bi, j: (bi, 0, j)),
            tok(SSM_BC),
            tok(LANES),
            pl.BlockSpec((1, SSM_HEADS, q), lambda bi, j: (bi, 0, j)),
            _const_spec((1, LANES)),
            _const_spec((SSM_HEADS, 1)),
            _const_spec((1, SSM_D_INNER)),
            _const_spec((1, SSM_D_INNER)),
            _const_spec((SSM_D_INNER, D_MODEL)),
            _const_spec((1, D_MODEL)),
            _const_spec((1, D_MODEL)),
        ],
        out_specs=tok(D_MODEL),
        out_shape=jax.ShapeDtypeStruct((TOKENS, D_MODEL), F32),
        scratch_shapes=[
            pltpu.VMEM((SSM_HEADS, SSM_STATE, SSM_HEAD_DIM), F32),
            pltpu.VMEM((q, SSM_D_INNER), F32),
        ],
        compiler_params=_params(("arbitrary", "arbitrary")),
        name="ssd_scan",
    )(x, z, xs, bm, bt, cm, dt, dtt, alog_row, alog_col, dskip, ng, w_out, g, b)


def _row(v):
    return v.reshape(1, -1).astype(F32)


def _mamba_layer(x, w_in, conv_w, conv_b, dt_bias, a_log, d_skip, norm_g, w_out, g, b):
    w_z = w_in[:, :SSM_D_INNER].astype(BF16)
    w_xbc = w_in[:, SSM_D_INNER:SSM_D_INNER + SSM_CONV_DIM].astype(BF16)
    w_dt = jnp.pad(w_in[:, SSM_D_INNER + SSM_CONV_DIM:], ((0, 0), (0, LANES - SSM_HEADS))).astype(BF16)
    dtb = jnp.pad(dt_bias, (0, LANES - SSM_HEADS)).reshape(1, LANES)
    z, xs, bm, bt, cm, dt, dtt = _ssm_in(x, w_z, w_xbc, w_dt, conv_w, _row(conv_b), dtb)
    alog_row = jnp.pad(a_log, (0, LANES - SSM_HEADS)).reshape(1, LANES)
    alog_col = a_log.reshape(SSM_HEADS, 1)
    dskip = jnp.repeat(d_skip, SSM_HEAD_DIM).reshape(1, SSM_D_INNER)
    return _ssd(x, z, xs, bm, bt, cm, dt, dtt, alog_row, alog_col, dskip, _row(norm_g),
                w_out.astype(BF16), _row(g), _row(b))


def _sg_layer(x, w_in, b_in, ln_g, ln_b, w_s, b_s, w_out, g, b):
    b_s_wide = jnp.repeat(b_s.T, SG_GROUP_DIM, axis=1)
    return _spatial_gating(x, w_in.astype(BF16), _row(b_in), _row(ln_g), _row(ln_b), w_s, b_s_wide,
                           w_out.astype(BF16), _row(g), _row(b))


def _mla_layer(x, positions, w_in, q_norm_g, w_q_b, kv_norm_g, w_kv_b, w_out, g, b):
    h = MLA_HEADS
    s0 = MLA_Q_RANK + MLA_KV_RANK
    t1 = w_in[:, s0:s0 + MLA_HALF]
    t2 = w_in[:, s0 + MLA_HALF:s0 + MLA_ROPE]
    zl = jnp.zeros((D_MODEL, MLA_NOPE), F32)
    zh = jnp.zeros((D_MODEL, LANES - MLA_QK), F32)
    w_in_pad = jnp.concatenate([w_in[:, :s0], zl, t1, t2, zh, zl, -t2, t1, zh], axis=1).astype(BF16)

    wq = w_q_b.reshape(MLA_Q_RANK, h, MLA_QK)
    wq = jnp.pad(wq, ((0, 0), (0, 0), (0, MLA_HEAD_PAD - MLA_QK)))
    wqt = wq.reshape(MLA_Q_RANK, h * MLA_HEAD_PAD).T.astype(BF16)
    wkv = w_kv_b.reshape(MLA_KV_RANK, h, MLA_NOPE + MLA_V)
    wk = jnp.pad(wkv[:, :, :MLA_NOPE], ((0, 0), (0, 0), (0, MLA_HEAD_PAD - MLA_NOPE)))
    wk = wk.reshape(MLA_KV_RANK, h * MLA_HEAD_PAD).astype(BF16)
    wvt = wkv[:, :, MLA_NOPE:].reshape(MLA_KV_RANK, h * MLA_V).T.astype(BF16)

    inv = (ROPE_THETA ** (-(jnp.arange(MLA_HALF, dtype=F32) * 2.0 / MLA_ROPE))).reshape(MLA_HALF, 1)
    cos_t, sin_t, cos_tab, sin_tab = _rope_tables(positions, inv)
    qt, k, vt = _mla_proj(x, w_in_pad, _row(q_norm_g), _row(kv_norm_g), wqt, wk, wvt,
                          cos_t, sin_t, cos_tab, sin_tab)
    o = _attention(qt, k, vt)
    return _proj_ln(x, o, w_out.astype(BF16), _row(g), _row(b))


def kernel(x, positions, ssm_w_in, ssm_conv_w, ssm_conv_b, ssm_dt_bias, ssm_a_log, ssm_d, ssm_norm_g, ssm_w_out, sg_w_in, sg_b_in, sg_ln_g, sg_ln_b, sg_w_s, sg_b_s, sg_w_out, mla_w_in, mla_q_norm_g, mla_w_q_b, mla_kv_norm_g, mla_w_kv_b, mla_w_out, ffn_w_in, ffn_conv_w, ffn_conv_b, ffn_w_out, ln_g, ln_b):
    h = x.reshape(TOKENS, D_MODEL)
    for i in range(DEPTH):
        m, j = i % N_MIXERS, i // N_MIXERS
        g0, b0 = ln_g[i, 0], ln_b[i, 0]
        if m == 0:
            h = _mamba_layer(h, ssm_w_in[j], ssm_conv_w[j], ssm_conv_b[j], ssm_dt_bias[j], ssm_a_log[j],
                             ssm_d[j], ssm_norm_g[j], ssm_w_out[j], g0, b0)
        elif m == 1:
            h = _sg_layer(h, sg_w_in[j], sg_b_in[j], sg_ln_g[j], sg_ln_b[j], sg_w_s[j], sg_b_s[j],
                          sg_w_out[j], g0, b0)
        else:
            h = _mla_layer(h, positions, mla_w_in[j], mla_q_norm_g[j], mla_w_q_b[j], mla_kv_norm_g[j],
                           mla_w_kv_b[j], mla_w_out[j], g0, b0)
        h = _ffn(h, ffn_w_in[i].astype(BF16), ffn_conv_w[i], _row(ffn_conv_b[i]),
                 ffn_w_out[i].astype(BF16), _row(ln_g[i, 1]), _row(ln_b[i, 1]))
    return h.reshape(BATCH, SEQ, D_MODEL)
```

```python
import functools
import math

import jax
import jax.numpy as jnp
from jax import lax
from jax.experimental import pallas as pl
from jax.experimental.pallas import tpu as pltpu

D_MODEL = 1024
BATCH = 8
SEQ = 4096
DEPTH = 4
TOKENS = BATCH * SEQ

CHUNK = 64
N_MIXERS = 3

DN_ALPHA = (2.0 * DEPTH) ** 0.25
LN_EPS = 1e-5
RMS_EPS = 1e-6

SSM_D_INNER = 2048
SSM_HEAD_DIM = 64
SSM_HEADS = 32
SSM_GROUPS = 8
SSM_HPG = 4
SSM_STATE = 128
SSM_CONV = 4
SSM_CONV_DIM = 4096
SSM_BC = SSM_GROUPS * SSM_STATE

SG_BLOCK = 128
SG_WIDTH = 2048
SG_GROUPS = 8
SG_GROUP_DIM = 256

MLA_HEADS = 16
MLA_Q_RANK = 384
MLA_KV_RANK = 256
MLA_NOPE = 64
MLA_ROPE = 32
MLA_HALF = 16
MLA_V = 64
MLA_QK = MLA_NOPE + MLA_ROPE
ROPE_THETA = 10000.0

FFN_HIDDEN = 2816
FFN_CONV = 3

LANES = 128
SUBLANES = 8
HALO = SUBLANES
VMEM_LIMIT = 56 * 1024 * 1024

BF16 = jnp.bfloat16
F32 = jnp.float32
NEG_BIG = -1e30


def _dot(a, b):
    return jnp.dot(a, b, preferred_element_type=F32)


def _dot_nt(a, b):
    return lax.dot_general(a, b, (((1,), (1,)), ((), ())), preferred_element_type=F32)


def _layer_norm(z, g, b):
    mu = jnp.mean(z, axis=-1, keepdims=True)
    zc = z - mu
    var = jnp.mean(zc * zc, axis=-1, keepdims=True)
    return zc * lax.rsqrt(var + LN_EPS) * g + b


def _silu(x):
    return x * (1.0 / (1.0 + jnp.exp(-x)))


def _const_spec(shape):
    nd = len(shape)
    return pl.BlockSpec(shape, lambda *_: (0,) * nd, pipeline_mode=pl.Buffered(1))


def _params(sem):
    return pltpu.CompilerParams(dimension_semantics=sem, vmem_limit_bytes=VMEM_LIMIT)


FFN_TM = 512
FFN_CHUNKS = ((0, 768), (768, 768), (1536, 768), (2304, 512))


def _ffn_kernel(x_ref, win_ref, cw_ref, cb_ref, wout_ref, g_ref, b_ref, o_ref, hs_ref, acc_ref):
    tm = FFN_TM
    i = pl.program_id(0)
    first = (i % (SEQ // tm)) == 0

    @pl.when(first)
    def _():
        hs_ref[:, 0:HALO, :] = jnp.zeros((2, HALO, FFN_HIDDEN), F32)

    @pl.when(jnp.logical_not(first))
    def _():
        hs_ref[:, 0:HALO, :] = hs_ref[:, tm:tm + HALO, :]

    x = x_ref[...]
    xb = x.astype(BF16)
    for ci, (c0, cw) in enumerate(FFN_CHUNKS):
        conv = []
        for half in range(2):
            off = half * FFN_HIDDEN + c0
            hs_ref[half, HALO:HALO + tm, c0:c0 + cw] = _dot(xb, win_ref[:, off:off + cw])
            c = cb_ref[:, off:off + cw]
            for k in range(FFN_CONV):
                r0 = HALO - (FFN_CONV - 1) + k
                c = c + cw_ref[k:k + 1, off:off + cw] * hs_ref[half, r0:r0 + tm, c0:c0 + cw]
            conv.append(c)
        act = (_silu(conv[0]) * conv[1]).astype(BF16)
        part = _dot(act, wout_ref[c0:c0 + cw, :])
        if ci == 0:
            acc_ref[...] = part
        else:
            acc_ref[...] += part
    z = DN_ALPHA * x + acc_ref[...]
    o_ref[...] = _layer_norm(z, g_ref[...], b_ref[...])


def _ffn(x, w_in, conv_w, conv_b, w_out, g, b):
    tm = FFN_TM
    return pl.pallas_call(
        _ffn_kernel,
        grid=(TOKENS // tm,),
        in_specs=[
            pl.BlockSpec((tm, D_MODEL), lambda i: (i, 0)),
            _const_spec((D_MODEL, 2 * FFN_HIDDEN)),
            _const_spec((FFN_CONV, 2 * FFN_HIDDEN)),
            _const_spec((1, 2 * FFN_HIDDEN)),
            _const_spec((FFN_HIDDEN, D_MODEL)),
            _const_spec((1, D_MODEL)),
            _const_spec((1, D_MODEL)),
        ],
        out_specs=pl.BlockSpec((tm, D_MODEL), lambda i: (i, 0)),
        out_shape=jax.ShapeDtypeStruct((TOKENS, D_MODEL), F32),
        scratch_shapes=[
            pltpu.VMEM((2, tm + HALO, FFN_HIDDEN), F32),
            pltpu.VMEM((tm, D_MODEL), F32),
        ],
        compiler_params=_params(("arbitrary",)),
        name="conv_ffn",
    )(x, w_in, conv_w, conv_b, w_out, g, b)


SG_TM = 512


def _sg_kernel(x_ref, win_ref, bin_ref, lng_ref, lnb_ref, ws_ref, bs_ref, wout_ref, g_ref, b_ref,
               o_ref, u_ref, v_ref, gated_ref):
    tm = SG_TM
    x = x_ref[...]
    xb = x.astype(BF16)
    u_ref[...] = jax.nn.gelu(_dot(xb, win_ref[:, 0:SG_WIDTH]) + bin_ref[:, 0:SG_WIDTH])
    v = jax.nn.gelu(_dot(xb, win_ref[:, SG_WIDTH:2 * SG_WIDTH]) + bin_ref[:, SG_WIDTH:2 * SG_WIDTH])
    v_ref[...] = _layer_norm(v, lng_ref[...], lnb_ref[...]).astype(BF16)

    row = lax.broadcasted_iota(jnp.int32, (SG_BLOCK, SG_BLOCK), 0)
    col = lax.broadcasted_iota(jnp.int32, (SG_BLOCK, SG_BLOCK), 1)
    causal = col <= row
    for grp in range(SG_GROUPS):
        c0 = grp * SG_GROUP_DIM
        ws = jnp.where(causal, ws_ref[grp], 0.0).astype(BF16)
        bias = bs_ref[:, c0:c0 + SG_GROUP_DIM]
        for blk in range(tm // SG_BLOCK):
            r0 = blk * SG_BLOCK
            mixed = _dot(ws, v_ref[r0:r0 + SG_BLOCK, c0:c0 + SG_GROUP_DIM]) + bias
            gated_ref[r0:r0 + SG_BLOCK, c0:c0 + SG_GROUP_DIM] = (
                u_ref[r0:r0 + SG_BLOCK, c0:c0 + SG_GROUP_DIM] * mixed).astype(BF16)
    z = DN_ALPHA * x + _dot(gated_ref[...], wout_ref[...])
    o_ref[...] = _layer_norm(z, g_ref[...], b_ref[...])


def _spatial_gating(x, w_in, b_in, ln_g, ln_b, w_s, b_s_wide, w_out, g, b):
    tm = SG_TM
    return pl.pallas_call(
        _sg_kernel,
        grid=(TOKENS // tm,),
        in_specs=[
            pl.BlockSpec((tm, D_MODEL), lambda i: (i, 0)),
            _const_spec((D_MODEL, 2 * SG_WIDTH)),
            _const_spec((1, 2 * SG_WIDTH)),
            _const_spec((1, SG_WIDTH)),
            _const_spec((1, SG_WIDTH)),
            _const_spec((SG_GROUPS, SG_BLOCK, SG_BLOCK)),
            _const_spec((SG_BLOCK, SG_WIDTH)),
            _const_spec((SG_WIDTH, D_MODEL)),
            _const_spec((1, D_MODEL)),
            _const_spec((1, D_MODEL)),
        ],
        out_specs=pl.BlockSpec((tm, D_MODEL), lambda i: (i, 0)),
        out_shape=jax.ShapeDtypeStruct((TOKENS, D_MODEL), F32),
        scratch_shapes=[
            pltpu.VMEM((tm, SG_WIDTH), F32),
            pltpu.VMEM((tm, SG_WIDTH), BF16),
            pltpu.VMEM((tm, SG_WIDTH), BF16),
        ],
        compiler_params=_params(("arbitrary",)),
        name="spatial_gating",
    )(x, w_in, b_in, ln_g, ln_b, w_s, b_s_wide, w_out, g, b)


ROPE_TL = 512
MLA_TM = 512
MLA_LAT = MLA_Q_RANK + MLA_KV_RANK + 2 * LANES
MLA_HEAD_PAD = LANES
ATT_T = 256
ATT_NQ = SEQ // ATT_T
ATT_HEADS_PER_STEP = 4
ATT_MASK_ROWS = 16
ATT_Q_SCALE = (MLA_QK ** -0.5) * math.log2(math.e)


def _rope_kernel(pos_ref, inv_ref, cos_t_ref, sin_t_ref, cos_tab_ref, sin_tab_ref):
    ang = inv_ref[...] * pos_ref[0].astype(F32)
    c = jnp.cos(ang)
    s = jnp.sin(ang)
    cos_t_ref[0] = c
    sin_t_ref[0] = s
    zlo = jnp.zeros((MLA_NOPE, ROPE_TL), F32)
    zhi = jnp.zeros((LANES - MLA_QK, ROPE_TL), F32)
    cos_tab_ref[...] = jnp.concatenate([zlo, c, c, zhi], axis=0).T
    sin_tab_ref[...] = jnp.concatenate([zlo, s, s, zhi], axis=0).T


def _rope_tables(positions, inv):
    nl = SEQ // ROPE_TL
    pos3 = positions.reshape(BATCH, 1, SEQ)
    return pl.pallas_call(
        _rope_kernel,
        grid=(BATCH, nl),
        in_specs=[
            pl.BlockSpec((1, 1, ROPE_TL), lambda b, j: (b, 0, j)),
            pl.BlockSpec((MLA_HALF, 1), lambda b, j: (0, 0)),
        ],
        out_specs=[
            pl.BlockSpec((1, MLA_HALF, ROPE_TL), lambda b, j: (b, 0, j)),
            pl.BlockSpec((1, MLA_HALF, ROPE_TL), lambda b, j: (b, 0, j)),
            pl.BlockSpec((ROPE_TL, LANES), lambda b, j: (b * nl + j, 0)),
            pl.BlockSpec((ROPE_TL, LANES), lambda b, j: (b * nl + j, 0)),
        ],
        out_shape=[
            jax.ShapeDtypeStruct((BATCH, MLA_HALF, SEQ), F32),
            jax.ShapeDtypeStruct((BATCH, MLA_HALF, SEQ), F32),
            jax.ShapeDtypeStruct((TOKENS, LANES), F32),
            jax.ShapeDtypeStruct((TOKENS, LANES), F32),
        ],
        compiler_params=_params(("arbitrary", "arbitrary")),
        name="rope_tables",
    )(pos3, inv)


def _rms(x, g):
    return x * lax.rsqrt(jnp.mean(x * x, axis=-1, keepdims=True) + RMS_EPS) * g


def _mla_proj_kernel(x_ref, win_ref, gq_ref, gkv_ref, wqt_ref, wk_ref, wvt_ref,
                     cos_t_ref, sin_t_ref, cos_tab_ref, sin_tab_ref,
                     qt_ref, k_ref, vt_ref):
    xb = x_ref[...].astype(BF16)
    lat = _dot(xb, win_ref[...])
    qn = _rms(lat[:, 0:MLA_Q_RANK], gq_ref[...]).astype(BF16)
    kvn = _rms(lat[:, MLA_Q_RANK:MLA_Q_RANK + MLA_KV_RANK], gkv_ref[...]).astype(BF16)
    s0 = MLA_Q_RANK + MLA_KV_RANK
    kslab = lat[:, s0:s0 + LANES] * cos_tab_ref[...] + lat[:, s0 + LANES:s0 + 2 * LANES] * sin_tab_ref[...]
    lane = lax.broadcasted_iota(jnp.int32, (MLA_TM, LANES), 1) - MLA_QK
    chunk = (lax.broadcasted_iota(jnp.int32, (MLA_TM, LANES), 0) // CHUNK) % (ATT_T // CHUNK)
    kslab = kslab + jnp.where(lane == chunk, 1.0, 0.0)
    kmat = _dot(kvn, wk_ref[...])
    for h in range(MLA_HEADS):
        c0 = h * MLA_HEAD_PAD
        k_ref[:, c0:c0 + MLA_HEAD_PAD] = (kmat[:, c0:c0 + MLA_HEAD_PAD] + kslab).astype(BF16)
    vt_ref[0] = _dot_nt(wvt_ref[...], kvn).astype(BF16)
    qt = _dot_nt(wqt_ref[...], qn) * ATT_Q_SCALE
    cos_t = cos_t_ref[0]
    sin_t = sin_t_ref[0]
    for h in range(MLA_HEADS):
        r0 = h * MLA_HEAD_PAD
        qt_ref[0, r0:r0 + MLA_NOPE, :] = qt[r0:r0 + MLA_NOPE, :].astype(BF16)
        t1 = qt[r0 + MLA_NOPE:r0 + MLA_NOPE + MLA_HALF, :]
        t2 = qt[r0 + MLA_NOPE + MLA_HALF:r0 + MLA_QK, :]
        qt_ref[0, r0 + MLA_NOPE:r0 + MLA_NOPE + MLA_HALF, :] = (t1 * cos_t - t2 * sin_t).astype(BF16)
        qt_ref[0, r0 + MLA_NOPE + MLA_HALF:r0 + MLA_QK, :] = (t2 * cos_t + t1 * sin_t).astype(BF16)
        qt_ref[0, r0 + MLA_QK:r0 + MLA_HEAD_PAD, :] = jnp.zeros((MLA_HEAD_PAD - MLA_QK, MLA_TM), BF16)


def _mla_proj(x, w_in_pad, gq, gkv, wqt, wk, wvt, cos_t, sin_t, cos_tab, sin_tab):
    tm = MLA_TM
    nl = SEQ // tm
    hp = MLA_HEADS * MLA_HEAD_PAD
    return pl.pallas_call(
        _mla_proj_kernel,
        grid=(BATCH, nl),
        in_specs=[
            pl.BlockSpec((tm, D_MODEL), lambda b, j: (b * nl + j, 0)),
            _const_spec((D_MODEL, MLA_LAT)),
            _const_spec((1, MLA_Q_RANK)),
            _const_spec((1, MLA_KV_RANK)),
            _const_spec((hp, MLA_Q_RANK)),
            _const_spec((MLA_KV_RANK, hp)),
            _const_spec((MLA_HEADS * MLA_V, MLA_KV_RANK)),
            pl.BlockSpec((1, MLA_HALF, tm), lambda b, j: (b, 0, j)),
            pl.BlockSpec((1, MLA_HALF, tm), lambda b, j: (b, 0, j)),
            pl.BlockSpec((tm, LANES), lambda b, j: (b * nl + j, 0)),
            pl.BlockSpec((tm, LANES), lambda b, j: (b * nl + j, 0)),
        ],
        out_specs=[
            pl.BlockSpec((1, hp, tm), lambda b, j: (b, 0, j)),
            pl.BlockSpec((tm, hp), lambda b, j: (b * nl + j, 0)),
            pl.BlockSpec((1, MLA_HEADS * MLA_V, tm), lambda b, j: (b, 0, j)),
        ],
        out_shape=[
            jax.ShapeDtypeStruct((BATCH, hp, SEQ), BF16),
            jax.ShapeDtypeStruct((TOKENS, hp), BF16),
            jax.ShapeDtypeStruct((BATCH, MLA_HEADS * MLA_V, SEQ), BF16),
        ],
        compiler_params=_params(("arbitrary", "arbitrary")),
        name="mla_proj",
    )(x, w_in_pad, gq, gkv, wqt, wk, wvt, cos_t, sin_t, cos_tab, sin_tab)


def _attn_kernel(qt_ref, k_ref, vt_ref, o_ref, q2_ref, s_ref, p_ref, acc_ref):
    t_ = ATT_T
    nh = ATT_HEADS_PER_STEP
    hp = MLA_HEAD_PAD

    q2_ref[0] = qt_ref[0]
    q2_ref[1] = qt_ref[0]
    crow = lax.broadcasted_iota(jnp.int32, (ATT_MASK_ROWS, t_), 0)
    qchunk = lax.broadcasted_iota(jnp.int32, (ATT_MASK_ROWS, t_), 1) // CHUNK
    mask_rows = jnp.where((crow < t_ // CHUNK) & (crow > qchunk), NEG_BIG, 0.0).astype(BF16)
    for hh in range(nh):
        r0 = hh * hp + MLA_QK
        for qq in range(ATT_NQ):
            q2_ref[1, r0:r0 + ATT_MASK_ROWS, qq * t_:(qq + 1) * t_] = mask_rows
    p_ref[...] = jnp.zeros(p_ref.shape, BF16)
    acc_ref[...] = jnp.zeros(acc_ref.shape, F32)

    def nxt(qi, j):
        last = j >= qi
        return jnp.where(last, qi + 1, qi), jnp.where(last, 0, j + 1)

    def qk(slot, qi, j):
        diag = (j == qi).astype(jnp.int32)
        k0 = pl.multiple_of(j * t_, t_)
        q0 = pl.multiple_of(qi * t_, t_)
        for hh in range(nh):
            kt = k_ref[pl.ds(k0, t_), hh * hp:(hh + 1) * hp]
            qv = q2_ref[diag, hh * hp:(hh + 1) * hp, pl.ds(q0, t_)]
            s_ref[slot, hh] = _dot(kt, qv)

    def sm(slot, j, m, l):
        fresh = j == 0
        m_out, l_out, a_out = [], [], []
        for hh in range(nh):
            s = s_ref[slot, hh]
            m_old = jnp.where(fresh, NEG_BIG, m[hh])
            l_old = jnp.where(fresh, 0.0, l[hh])
            m_new = jnp.maximum(m_old, jnp.max(s, axis=0, keepdims=True))
            alpha = jnp.exp2(m_old - m_new)
            p = jnp.exp2(s - m_new)
            l_out.append(alpha * l_old + jnp.sum(p, axis=0, keepdims=True))
            m_out.append(m_new)
            a_out.append(alpha)
            p_ref[slot, hh] = p.astype(BF16)
        return tuple(m_out), tuple(l_out), tuple(a_out)

    def pv(slot, qi, j, alpha, l):
        k0 = pl.multiple_of(j * t_, t_)
        q0 = pl.multiple_of(qi * t_, t_)
        for hh in range(nh):
            vt = vt_ref[0, hh * MLA_V:(hh + 1) * MLA_V, pl.ds(k0, t_)]
            upd = _dot(vt, p_ref[slot, hh])
            acc_old = jnp.where(j == 0, 0.0, acc_ref[hh])
            acc_new = acc_old * alpha[hh] + upd
            acc_ref[hh] = acc_new
            o_ref[0, hh * MLA_V:(hh + 1) * MLA_V, pl.ds(q0, t_)] = (acc_new * (1.0 / l[hh])).astype(BF16)

    def step(par, carry, do_qk=True, do_sm=True):
        qa, ja, qb, jb, qc, jc, m, l, al = carry
        pv(par, qc, jc, al, l)
        if do_sm:
            m2, l2, al2 = sm(1 - par, jb, m, l)
        else:
            m2, l2, al2 = m, l, al
        if do_qk:
            qk(par, qa, ja)
        qa2, ja2 = nxt(qa, ja)
        return (qa2, ja2, qa, ja, qb, jb, m2, l2, al2)

    zero = jnp.int32(0)
    qk(0, zero, zero)
    row = lambda v: tuple(jnp.full((1, t_), v, F32) for _ in range(nh))
    carry = (jnp.int32(1), zero, zero, zero, zero, zero, row(NEG_BIG), row(1.0), row(1.0))
    n_items = ATT_NQ * (ATT_NQ + 1) // 2
    assert n_items % 2 == 0
    carry = lax.fori_loop(0, (n_items - 2) // 2, lambda u, cr: step(0, step(1, cr)), carry)
    carry = step(1, carry)
    carry = step(0, carry, do_qk=False)
    step(1, carry, do_qk=False, do_sm=False)


def _attention(qt, k, vt):
    nh = ATT_HEADS_PER_STEP
    return pl.pallas_call(
        _attn_kernel,
        grid=(BATCH, MLA_HEADS // nh),
        in_specs=[
            pl.BlockSpec((1, nh * MLA_HEAD_PAD, SEQ), lambda b, h: (b, h, 0)),
            pl.BlockSpec((SEQ, nh * MLA_HEAD_PAD), lambda b, h: (b, h)),
            pl.BlockSpec((1, nh * MLA_V, SEQ), lambda b, h: (b, h, 0)),
        ],
        out_specs=pl.BlockSpec((1, nh * MLA_V, SEQ), lambda b, h: (b, h, 0)),
        out_shape=jax.ShapeDtypeStruct((BATCH, MLA_HEADS * MLA_V, SEQ), BF16),
        scratch_shapes=[
            pltpu.VMEM((2, nh * MLA_HEAD_PAD, SEQ), BF16),
            pltpu.VMEM((2, nh, ATT_T, ATT_T), F32),
            pltpu.VMEM((2, nh, ATT_T, ATT_T), BF16),
            pltpu.VMEM((nh, MLA_V, ATT_T), F32),
        ],
        compiler_params=_params(("arbitrary", "arbitrary")),
        name="mla_attention",
    )(qt, k, vt)


PROJ_TM = 512


def _proj_ln_kernel(x_ref, at_ref, w_ref, g_ref, b_ref, o_ref):
    y = lax.dot_general(at_ref[0], w_ref[...], (((0,), (0,)), ((), ())), preferred_element_type=F32)
    z = DN_ALPHA * x_ref[...] + y
    o_ref[...] = _layer_norm(z, g_ref[...], b_ref[...])


def _proj_ln(x, a_t, w, g, b):
    tm = PROJ_TM
    nl = SEQ // tm
    kdim = a_t.shape[1]
    return pl.pallas_call(
        _proj_ln_kernel,
        grid=(BATCH, nl),
        in_specs=[
            pl.BlockSpec((tm, D_MODEL), lambda bi, j: (bi * nl + j, 0)),
            pl.BlockSpec((1, kdim, tm), lambda bi, j: (bi, 0, j)),
            _const_spec((kdim, D_MODEL)),
            _const_spec((1, D_MODEL)),
            _const_spec((1, D_MODEL)),
        ],
        out_specs=pl.BlockSpec((tm, D_MODEL), lambda bi, j: (bi * nl + j, 0)),
        out_shape=jax.ShapeDtypeStruct((TOKENS, D_MODEL), F32),
        compiler_params=_params(("arbitrary", "arbitrary")),
        name="proj_ln",
    )(x, a_t, w, g, b)


SSM_TM = 512
SSD_Q = 128


def _ssm_in_kernel(x_ref, wz_ref, wxbc_ref, wdt_ref, cw_ref, cb_ref, dtb_ref,
                   z_ref, xs_ref, b_ref, bt_ref, c_ref, dt_ref, dtt_ref, hs_ref):
    tm = SSM_TM
    first = pl.program_id(1) == 0

    @pl.when(first)
    def _():
        hs_ref[0:HALO, :] = jnp.zeros((HALO, SSM_CONV_DIM), F32)

    @pl.when(jnp.logical_not(first))
    def _():
        hs_ref[0:HALO, :] = hs_ref[tm:tm + HALO, :]

    xb = x_ref[...].astype(BF16)
    z_ref[...] = _dot(xb, wz_ref[...]).astype(BF16)
    hs_ref[HALO:HALO + tm, :] = _dot(xb, wxbc_ref[...])
    dt = jax.nn.softplus(_dot(xb, wdt_ref[...]) + dtb_ref[...])
    dt_ref[...] = dt
    dtt_ref[0] = dt.T[0:SSM_HEADS, :]

    def conv(c0, cw):
        acc = cb_ref[:, c0:c0 + cw]
        for k in range(SSM_CONV):
            r0 = HALO - (SSM_CONV - 1) + k
            acc = acc + cw_ref[k:k + 1, c0:c0 + cw] * hs_ref[r0:r0 + tm, c0:c0 + cw]
        return _silu(acc)

    half = SSM_D_INNER // 2
    xs_ref[:, 0:half] = conv(0, half).astype(BF16)
    xs_ref[:, half:SSM_D_INNER] = conv(half, half).astype(BF16)
    bm = conv(SSM_D_INNER, SSM_BC)
    b_ref[...] = bm.astype(BF16)
    bt_ref[0] = bm.T.astype(BF16)
    c_ref[...] = conv(SSM_D_INNER + SSM_BC, SSM_BC).astype(BF16)


def _ssm_in(x, w_z, w_xbc, w_dt, conv_w, conv_b, dt_bias):
    tm = SSM_TM
    nl = SEQ // tm
    tok = lambda n: pl.BlockSpec((tm, n), lambda b, j: (b * nl + j, 0))
    return pl.pallas_call(
        _ssm_in_kernel,
        grid=(BATCH, nl),
        in_specs=[
            tok(D_MODEL),
            _const_spec((D_MODEL, SSM_D_INNER)),
            _const_spec((D_MODEL, SSM_CONV_DIM)),
            _const_spec((D_MODEL, LANES)),
            _const_spec((SSM_CONV, SSM_CONV_DIM)),
            _const_spec((1, SSM_CONV_DIM)),
            _const_spec((1, LANES)),
        ],
        out_specs=[
            tok(SSM_D_INNER),
            tok(SSM_D_INNER),
            tok(SSM_BC),
            pl.BlockSpec((1, SSM_BC, tm), lambda b, j: (b, 0, j)),
            tok(SSM_BC),
            tok(LANES),
            pl.BlockSpec((1, SSM_HEADS, tm), lambda b, j: (b, 0, j)),
        ],
        out_shape=[
            jax.ShapeDtypeStruct((TOKENS, SSM_D_INNER), BF16),
            jax.ShapeDtypeStruct((TOKENS, SSM_D_INNER), BF16),
            jax.ShapeDtypeStruct((TOKENS, SSM_BC), BF16),
            jax.ShapeDtypeStruct((BATCH, SSM_BC, SEQ), BF16),
            jax.ShapeDtypeStruct((TOKENS, SSM_BC), BF16),
            jax.ShapeDtypeStruct((TOKENS, LANES), F32),
            jax.ShapeDtypeStruct((BATCH, SSM_HEADS, SEQ), F32),
        ],
        scratch_shapes=[pltpu.VMEM((tm + HALO, SSM_CONV_DIM), F32)],
        compiler_params=_params(("arbitrary", "arbitrary")),
        name="ssm_in",
    )(x, w_z, w_xbc, w_dt, conv_w, conv_b, dt_bias)


def _ssd_kernel(x_ref, z_ref, xs_ref, b_ref, bt_ref, c_ref, dt_ref, dtt_ref,
                alog_row_ref, alog_col_ref, dskip_ref, ng_ref, wout_ref, g_ref, bb_ref,
                o_ref, state_ref, y_ref):
    q = SSD_Q
    hi = lax.Precision.HIGHEST

    @pl.when(pl.program_id(1) == 0)
    def _():
        state_ref[...] = jnp.zeros(state_ref.shape, F32)

    row = lax.broadcasted_iota(jnp.int32, (q, q), 0)
    col = lax.broadcasted_iota(jnp.int32, (q, q), 1)
    lower = col <= row
    tri = lower.astype(F32)
    a = dt_ref[...] * (-jnp.exp(alog_row_ref[...]))
    a_t = dtt_ref[0] * (-jnp.exp(alog_col_ref[...]))
    acum = jnp.dot(tri, a, precision=hi, preferred_element_type=F32)
    acum_t = lax.dot_general(a_t, tri, (((1,), (1,)), ((), ())), precision=hi,
                             preferred_element_type=F32)
    e_col = jnp.exp(acum)
    w_t = jnp.exp(acum_t[:, q - 1:q] - acum_t) * dtt_ref[0]
    cd = jnp.exp(acum_t[:, q - 1:q])
    dt_t = dtt_ref[0]

    for grp in range(SSM_GROUPS):
        n0 = grp * SSM_STATE
        cg = c_ref[:, n0:n0 + SSM_STATE]
        cb = _dot_nt(cg, b_ref[:, n0:n0 + SSM_STATE])
        cgf = cg.astype(F32)
        btf = bt_ref[0, n0:n0 + SSM_STATE, :].astype(F32)
        for r in range(SSM_HPG):
            h = grp * SSM_HPG + r
            p0 = h * SSM_HEAD_DIM
            xh = xs_ref[:, p0:p0 + SSM_HEAD_DIM]
            seg = acum[:, h:h + 1] - acum_t[h:h + 1, :]
            lmat = jnp.exp(jnp.where(lower, seg, NEG_BIG))
            mh = (cb * lmat * dt_t[h:h + 1, :]).astype(BF16)
            ec = (cgf * e_col[:, h:h + 1]).astype(BF16)
            s_old = state_ref[h]
            lhs = jnp.concatenate([mh, ec], axis=1)
            rhs = jnp.concatenate([xh, s_old.astype(BF16)], axis=0)
            yh = _dot(lhs, rhs) + dskip_ref[:, p0:p0 + SSM_HEAD_DIM] * xh.astype(F32)
            y_ref[:, p0:p0 + SSM_HEAD_DIM] = yh
            btw = (btf * w_t[h:h + 1, :]).astype(BF16)
            state_ref[h] = s_old * cd[h:h + 1, :] + _dot(btw, xh)

    zf = z_ref[...].astype(F32)
    y = y_ref[...] * _silu(zf)
    for grp in range(SSM_GROUPS):
        c0 = grp * SSM_HPG * SSM_HEAD_DIM
        cw = SSM_HPG * SSM_HEAD_DIM
        yg = y[:, c0:c0 + cw]
        y_ref[:, c0:c0 + cw] = _rms(yg, ng_ref[:, c0:c0 + cw])
    out = _dot(y_ref[...].astype(BF16), wout_ref[...])
    zres = DN_ALPHA * x_ref[...] + out
    o_ref[...] = _layer_norm(zres, g_ref[...], bb_ref[...])


def _ssd(x, z, xs, bm, bt, cm, dt, dtt, alog_row, alog_col, dskip, ng, w_out, g, b):
    q = SSD_Q
    nl = SEQ // q
    tok = lambda n: pl.BlockSpec((q, n), lambda bi, j: (bi * nl + j, 0))
    return pl.pallas_call(
        _ssd_kernel,
        grid=(BATCH, nl),
        in_specs=[
            tok(D_MODEL),
            tok(SSM_D_INNER),
            tok(SSM_D_INNER),
            tok(SSM_BC),
            pl.BlockSpec((1, SSM_BC, q), lambda bi, j: (bi, 0, j)),
            tok(SSM_BC),
            tok(LANES),
            pl.BlockSpec((1, SSM_HEADS, q), lambda bi, j: (bi, 0, j)),
            _const_spec((1, LANES)),
            _const_spec((SSM_HEADS, 1)),
            _const_spec((1, SSM_D_INNER)),
            _const_spec((1, SSM_D_INNER)),
            _const_spec((SSM_D_INNER, D_MODEL)),
            _const_spec((1, D_MODEL)),
            _const_spec((1, D_MODEL)),
        ],
        out_specs=tok(D_MODEL),
        out_shape=jax.ShapeDtypeStruct((TOKENS, D_MODEL), F32),
        scratch_shapes=[
            pltpu.VMEM((SSM_HEADS, SSM_STATE, SSM_HEAD_DIM), F32),
            pltpu.VMEM((q, SSM_D_INNER), F32),
        ],
        compiler_params=_params(("arbitrary", "arbitrary")),
        name="ssd_scan",
    )(x, z, xs, bm, bt, cm, dt, dtt, alog_row, alog_col, dskip, ng, w_out, g, b)


def _row(v):
    return v.reshape(1, -1).astype(F32)


def _mamba_layer(x, w_in, conv_w, conv_b, dt_bias, a_log, d_skip, norm_g, w_out, g, b):
    w_z = w_in[:, :SSM_D_INNER].astype(BF16)
    w_xbc = w_in[:, SSM_D_INNER:SSM_D_INNER + SSM_CONV_DIM].astype(BF16)
    w_dt = jnp.pad(w_in[:, SSM_D_INNER + SSM_CONV_DIM:], ((0, 0), (0, LANES - SSM_HEADS))).astype(BF16)
    dtb = jnp.pad(dt_bias, (0, LANES - SSM_HEADS)).reshape(1, LANES)
    z, xs, bm, bt, cm, dt, dtt = _ssm_in(x, w_z, w_xbc, w_dt, conv_w, _row(conv_b), dtb)
    alog_row = jnp.pad(a_log, (0, LANES - SSM_HEADS)).reshape(1, LANES)
    alog_col = a_log.reshape(SSM_HEADS, 1)
    dskip = jnp.repeat(d_skip, SSM_HEAD_DIM).reshape(1, SSM_D_INNER)
    return _ssd(x, z, xs, bm, bt, cm, dt, dtt, alog_row, alog_col, dskip, _row(norm_g),
                w_out.astype(BF16), _row(g), _row(b))


def _sg_layer(x, w_in, b_in, ln_g, ln_b, w_s, b_s, w_out, g, b):
    b_s_wide = jnp.repeat(b_s.T, SG_GROUP_DIM, axis=1)
    return _spatial_gating(x, w_in.astype(BF16), _row(b_in), _row(ln_g), _row(ln_b), w_s, b_s_wide,
                           w_out.astype(BF16), _row(g), _row(b))


def _mla_layer(x, positions, w_in, q_norm_g, w_q_b, kv_norm_g, w_kv_b, w_out, g, b):
    h = MLA_HEADS
    s0 = MLA_Q_RANK + MLA_KV_RANK
    t1 = w_in[:, s0:s0 + MLA_HALF]
    t2 = w_in[:, s0 + MLA_HALF:s0 + MLA_ROPE]
    zl = jnp.zeros((D_MODEL, MLA_NOPE), F32)
    zh = jnp.zeros((D_MODEL, LANES - MLA_QK), F32)
    w_in_pad = jnp.concatenate([w_in[:, :s0], zl, t1, t2, zh, zl, -t2, t1, zh], axis=1).astype(BF16)

    wq = w_q_b.reshape(MLA_Q_RANK, h, MLA_QK)
    wq = jnp.pad(wq, ((0, 0), (0, 0), (0, MLA_HEAD_PAD - MLA_QK)))
    wqt = wq.reshape(MLA_Q_RANK, h * MLA_HEAD_PAD).T.astype(BF16)
    wkv = w_kv_b.reshape(MLA_KV_RANK, h, MLA_NOPE + MLA_V)
    wk = jnp.pad(wkv[:, :, :MLA_NOPE], ((0, 0), (0, 0), (0, MLA_HEAD_PAD - MLA_NOPE)))
    wk = wk.reshape(MLA_KV_RANK, h * MLA_HEAD_PAD).astype(BF16)
    wvt = wkv[:, :, MLA_NOPE:].reshape(MLA_KV_RANK, h * MLA_V).T.astype(BF16)

    inv = (ROPE_THETA ** (-(jnp.arange(MLA_HALF, dtype=F32) * 2.0 / MLA_ROPE))).reshape(MLA_HALF, 1)
    cos_t, sin_t, cos_tab, sin_tab = _rope_tables(positions, inv)
    qt, k, vt = _mla_proj(x, w_in_pad, _row(q_norm_g), _row(kv_norm_g), wqt, wk, wvt,
                          cos_t, sin_t, cos_tab, sin_tab)
    o = _attention(qt, k, vt)
    return _proj_ln(x, o, w_out.astype(BF16), _row(g), _row(b))


def kernel(x, positions, ssm_w_in, ssm_conv_w, ssm_conv_b, ssm_dt_bias, ssm_a_log, ssm_d, ssm_norm_g, ssm_w_out, sg_w_in, sg_b_in, sg_ln_g, sg_ln_b, sg_w_s, sg_b_s, sg_w_out, mla_w_in, mla_q_norm_g, mla_w_q_b, mla_kv_norm_g, mla_w_kv_b, mla_w_out, ffn_w_in, ffn_conv_w, ffn_conv_b, ffn_w_out, ln_g, ln_b):
    h = x.reshape(TOKENS, D_MODEL)
    for i in range(DEPTH):
        m, j = i % N_MIXERS, i // N_MIXERS
        g0, b0 = ln_g[i, 0], ln_b[i, 0]
        if m == 0:
            h = _mamba_layer(h, ssm_w_in[j], ssm_conv_w[j], ssm_conv_b[j], ssm_dt_bias[j], ssm_a_log[j],
                             ssm_d[j], ssm_norm_g[j], ssm_w_out[j], g0, b0)
        elif m == 1:
            h = _sg_layer(h, sg_w_in[j], sg_b_in[j], sg_ln_g[j], sg_ln_b[j], sg_w_s[j], sg_b_s[j],
                          sg_w_out[j], g0, b0)
        else:
            h = _mla_layer(h, positions, mla_w_in[j], mla_q_norm_g[j], mla_w_q_b[j], mla_kv_norm_g[j],
                           mla_w_kv_b[j], mla_w_out[j], g0, b0)
        h = _ffn(h, ffn_w_in[i].astype(BF16), ffn_conv_w[i], _row(ffn_conv_b[i]),
                 ffn_w_out[i].astype(BF16), _row(ln_g[i, 1]), _row(ln_b[i, 1]))
    return h.reshape(BATCH, SEQ, D_MODEL)
```

```python
import functools
import math

import jax
import jax.numpy as jnp
from jax import lax
from jax.experimental import pallas as pl
from jax.experimental.pallas import tpu as pltpu

D_MODEL = 1024
BATCH = 8
SEQ = 4096
DEPTH = 4
TOKENS = BATCH * SEQ

CHUNK = 64
N_MIXERS = 3

DN_ALPHA = (2.0 * DEPTH) ** 0.25
LN_EPS = 1e-5
RMS_EPS = 1e-6

SSM_D_INNER = 2048
SSM_HEAD_DIM = 64
SSM_HEADS = 32
SSM_GROUPS = 8
SSM_HPG = 4
SSM_STATE = 128
SSM_CONV = 4
SSM_CONV_DIM = 4096
SSM_BC = SSM_GROUPS * SSM_STATE

SG_BLOCK = 128
SG_WIDTH = 2048
SG_GROUPS = 8
SG_GROUP_DIM = 256

MLA_HEADS = 16
MLA_Q_RANK = 384
MLA_KV_RANK = 256
MLA_NOPE = 64
MLA_ROPE = 32
MLA_HALF = 16
MLA_V = 64
MLA_QK = MLA_NOPE + MLA_ROPE
ROPE_THETA = 10000.0

FFN_HIDDEN = 2816
FFN_CONV = 3

LANES = 128
SUBLANES = 8
HALO = SUBLANES
VMEM_LIMIT = 56 * 1024 * 1024

BF16 = jnp.bfloat16
F32 = jnp.float32
NEG_BIG = -1e30


def _dot(a, b):
    return jnp.dot(a, b, preferred_element_type=F32)


def _dot_nt(a, b):
    return lax.dot_general(a, b, (((1,), (1,)), ((), ())), preferred_element_type=F32)


def _layer_norm(z, g, b):
    mu = jnp.mean(z, axis=-1, keepdims=True)
    zc = z - mu
    var = jnp.mean(zc * zc, axis=-1, keepdims=True)
    return zc * lax.rsqrt(var + LN_EPS) * g + b


def _silu(x):
    return x * (1.0 / (1.0 + jnp.exp(-x)))


def _const_spec(shape):
    nd = len(shape)
    return pl.BlockSpec(shape, lambda *_: (0,) * nd, pipeline_mode=pl.Buffered(1))


def _params(sem):
    return pltpu.CompilerParams(dimension_semantics=sem, vmem_limit_bytes=VMEM_LIMIT)


FFN_TM = 512
CONV_BLK = 2 * LANES
FFN_NBLK = FFN_HIDDEN // CONV_BLK
FFN_OUT_GROUP = 3
FFN_NGROUP = -(-FFN_NBLK // FFN_OUT_GROUP)
FFN_ROW_SPLIT = 1


def _conv_block(buf, halo_ref, cw_ref, cb_ref, c0, taps, tm):
    buf[0:HALO, :] = halo_ref[:, c0:c0 + CONV_BLK]
    halo_ref[:, c0:c0 + CONV_BLK] = buf[tm:tm + HALO, :]
    acc = cb_ref[:, c0:c0 + CONV_BLK]
    for k in range(taps):
        r0 = HALO - (taps - 1) + k
        acc = acc + cw_ref[k:k + 1, c0:c0 + CONV_BLK] * buf[r0:r0 + tm, :]
    return acc


def _ffn_kernel(x_ref, win_ref, cw_ref, cb_ref, wout_ref, g_ref, b_ref, o_ref,
                buf_ref, halo_ref, act_ref, acc_ref, xb_ref):
    tm = FFN_TM

    @pl.when(pl.program_id(0) % (SEQ // tm) == 0)
    def _():
        halo_ref[...] = jnp.zeros(halo_ref.shape, F32)

    xb_ref[...] = x_ref[...].astype(BF16)
    rb = tm // FFN_ROW_SPLIT

    def project(k, s):
        rows = slice(s * rb, (s + 1) * rb)
        for half in range(2):
            off = half * FFN_HIDDEN + k * CONV_BLK
            buf_ref[k % 2, half, HALO + s * rb:HALO + (s + 1) * rb, :] = _dot(xb_ref[rows, :], win_ref[:, off:off + CONV_BLK])

    def gate(k, s):
        conv = []
        for half in range(2):
            c0 = half * FFN_HIDDEN + k * CONV_BLK
            buf = buf_ref.at[k % 2, half]
            if s == 0:
                buf[0:HALO, :] = halo_ref[:, c0:c0 + CONV_BLK]
            if s == FFN_ROW_SPLIT - 1:
                halo_ref[:, c0:c0 + CONV_BLK] = buf[tm:tm + HALO, :]
            acc = cb_ref[:, c0:c0 + CONV_BLK]
            for t in range(FFN_CONV):
                r0 = HALO - (FFN_CONV - 1) + t + s * rb
                acc = acc + cw_ref[t:t + 1, c0:c0 + CONV_BLK] * buf[r0:r0 + rb, :]
            conv.append(acc)
        j = k % FFN_OUT_GROUP
        act_ref[k // FFN_OUT_GROUP, s * rb:(s + 1) * rb, j * CONV_BLK:(j + 1) * CONV_BLK] = (
            _silu(conv[0]) * conv[1]).astype(BF16)

    def contract(grp, s):
        rows = slice(s * rb, (s + 1) * rb)
        k0 = grp * FFN_OUT_GROUP
        n = min(FFN_OUT_GROUP, FFN_NBLK - k0) * CONV_BLK
        part = _dot(act_ref[grp, rows, 0:n], wout_ref[k0 * CONV_BLK:k0 * CONV_BLK + n, :])
        if grp == 0:
            acc_ref[rows, :] = part
        else:
            acc_ref[rows, :] += part

    for s in range(FFN_ROW_SPLIT):
        project(0, s)
    for k in range(FFN_NBLK):
        for s in range(FFN_ROW_SPLIT):
            if k + 1 < FFN_NBLK:
                project(k + 1, s)
            gate(k, s)
        if (k + 1) % FFN_OUT_GROUP == 0 or k + 1 == FFN_NBLK:
            for s in range(FFN_ROW_SPLIT):
                contract(k // FFN_OUT_GROUP, s)
    z = DN_ALPHA * x_ref[...] + acc_ref[...]
    o_ref[...] = _layer_norm(z, g_ref[...], b_ref[...])


def _ffn(x, w_in, conv_w, conv_b, w_out, g, b):
    tm = FFN_TM
    return pl.pallas_call(
        _ffn_kernel,
        grid=(TOKENS // tm,),
        in_specs=[
            pl.BlockSpec((tm, D_MODEL), lambda i: (i, 0)),
            _const_spec((D_MODEL, 2 * FFN_HIDDEN)),
            _const_spec((FFN_CONV, 2 * FFN_HIDDEN)),
            _const_spec((1, 2 * FFN_HIDDEN)),
            _const_spec((FFN_HIDDEN, D_MODEL)),
            _const_spec((1, D_MODEL)),
            _const_spec((1, D_MODEL)),
        ],
        out_specs=pl.BlockSpec((tm, D_MODEL), lambda i: (i, 0)),
        out_shape=jax.ShapeDtypeStruct((TOKENS, D_MODEL), F32),
        scratch_shapes=[
            pltpu.VMEM((2, 2, tm + HALO, CONV_BLK), F32),
            pltpu.VMEM((HALO, 2 * FFN_HIDDEN), F32),
            pltpu.VMEM((FFN_NGROUP, tm, FFN_OUT_GROUP * CONV_BLK), BF16),
            pltpu.VMEM((tm, D_MODEL), F32),
            pltpu.VMEM((tm, D_MODEL), BF16),
        ],
        compiler_params=_params(("arbitrary",)),
        name="conv_ffn",
    )(x, w_in, conv_w, conv_b, w_out, g, b)


SG_TM = 512


def _sg_kernel(x_ref, win_ref, bin_ref, lng_ref, lnb_ref, ws_ref, bs_ref, wout_ref, g_ref, b_ref,
               o_ref, u_ref, v_ref, gated_ref):
    tm = SG_TM
    x = x_ref[...]
    xb = x.astype(BF16)
    u_ref[...] = jax.nn.gelu(_dot(xb, win_ref[:, 0:SG_WIDTH]) + bin_ref[:, 0:SG_WIDTH])
    v = jax.nn.gelu(_dot(xb, win_ref[:, SG_WIDTH:2 * SG_WIDTH]) + bin_ref[:, SG_WIDTH:2 * SG_WIDTH])
    v_ref[...] = _layer_norm(v, lng_ref[...], lnb_ref[...]).astype(BF16)

    row = lax.broadcasted_iota(jnp.int32, (SG_BLOCK, SG_BLOCK), 0)
    col = lax.broadcasted_iota(jnp.int32, (SG_BLOCK, SG_BLOCK), 1)
    causal = col <= row
    for grp in range(SG_GROUPS):
        c0 = grp * SG_GROUP_DIM
        ws = jnp.where(causal, ws_ref[grp], 0.0).astype(BF16)
        bias = bs_ref[:, c0:c0 + SG_GROUP_DIM]
        for blk in range(tm // SG_BLOCK):
            r0 = blk * SG_BLOCK
            mixed = _dot(ws, v_ref[r0:r0 + SG_BLOCK, c0:c0 + SG_GROUP_DIM]) + bias
            gated_ref[r0:r0 + SG_BLOCK, c0:c0 + SG_GROUP_DIM] = (
                u_ref[r0:r0 + SG_BLOCK, c0:c0 + SG_GROUP_DIM] * mixed).astype(BF16)
    z = DN_ALPHA * x + _dot(gated_ref[...], wout_ref[...])
    o_ref[...] = _layer_norm(z, g_ref[...], b_ref[...])


def _spatial_gating(x, w_in, b_in, ln_g, ln_b, w_s, b_s_wide, w_out, g, b):
    tm = SG_TM
    return pl.pallas_call(
        _sg_kernel,
        grid=(TOKENS // tm,),
        in_specs=[
            pl.BlockSpec((tm, D_MODEL), lambda i: (i, 0)),
            _const_spec((D_MODEL, 2 * SG_WIDTH)),
            _const_spec((1, 2 * SG_WIDTH)),
            _const_spec((1, SG_WIDTH)),
            _const_spec((1, SG_WIDTH)),
            _const_spec((SG_GROUPS, SG_BLOCK, SG_BLOCK)),
            _const_spec((SG_BLOCK, SG_WIDTH)),
            _const_spec((SG_WIDTH, D_MODEL)),
            _const_spec((1, D_MODEL)),
            _const_spec((1, D_MODEL)),
        ],
        out_specs=pl.BlockSpec((tm, D_MODEL), lambda i: (i, 0)),
        out_shape=jax.ShapeDtypeStruct((TOKENS, D_MODEL), F32),
        scratch_shapes=[
            pltpu.VMEM((tm, SG_WIDTH), F32),
            pltpu.VMEM((tm, SG_WIDTH), BF16),
            pltpu.VMEM((tm, SG_WIDTH), BF16),
        ],
        compiler_params=_params(("arbitrary",)),
        name="spatial_gating",
    )(x, w_in, b_in, ln_g, ln_b, w_s, b_s_wide, w_out, g, b)


ROPE_TL = 512
MLA_TM = 512
MLA_LAT = MLA_Q_RANK + MLA_KV_RANK + 2 * LANES
MLA_HEAD_PAD = LANES
ATT_T = 256
ATT_NQ = SEQ // ATT_T
ATT_HEADS_PER_STEP = 4
ATT_MASK_ROWS = 16
ATT_Q_SCALE = (MLA_QK ** -0.5) * math.log2(math.e)


def _rope_kernel(pos_ref, inv_ref, cos_t_ref, sin_t_ref, cos_tab_ref, sin_tab_ref):
    ang = inv_ref[...] * pos_ref[0].astype(F32)
    c = jnp.cos(ang)
    s = jnp.sin(ang)
    cos_t_ref[0] = c
    sin_t_ref[0] = s
    zlo = jnp.zeros((MLA_NOPE, ROPE_TL), F32)
    zhi = jnp.zeros((LANES - MLA_QK, ROPE_TL), F32)
    cos_tab_ref[...] = jnp.concatenate([zlo, c, c, zhi], axis=0).T
    sin_tab_ref[...] = jnp.concatenate([zlo, s, s, zhi], axis=0).T


def _rope_tables(positions, inv):
    nl = SEQ // ROPE_TL
    pos3 = positions.reshape(BATCH, 1, SEQ)
    return pl.pallas_call(
        _rope_kernel,
        grid=(BATCH, nl),
        in_specs=[
            pl.BlockSpec((1, 1, ROPE_TL), lambda b, j: (b, 0, j)),
            pl.BlockSpec((MLA_HALF, 1), lambda b, j: (0, 0)),
        ],
        out_specs=[
            pl.BlockSpec((1, MLA_HALF, ROPE_TL), lambda b, j: (b, 0, j)),
            pl.BlockSpec((1, MLA_HALF, ROPE_TL), lambda b, j: (b, 0, j)),
            pl.BlockSpec((ROPE_TL, LANES), lambda b, j: (b * nl + j, 0)),
            pl.BlockSpec((ROPE_TL, LANES), lambda b, j: (b * nl + j, 0)),
        ],
        out_shape=[
            jax.ShapeDtypeStruct((BATCH, MLA_HALF, SEQ), F32),
            jax.ShapeDtypeStruct((BATCH, MLA_HALF, SEQ), F32),
            jax.ShapeDtypeStruct((TOKENS, LANES), F32),
            jax.ShapeDtypeStruct((TOKENS, LANES), F32),
        ],
        compiler_params=_params(("arbitrary", "arbitrary")),
        name="rope_tables",
    )(pos3, inv)


def _rms(x, g):
    return x * lax.rsqrt(jnp.mean(x * x, axis=-1, keepdims=True) + RMS_EPS) * g


def _mla_proj_kernel(x_ref, win_ref, gq_ref, gkv_ref, wqt_ref, wk_ref, wvt_ref,
                     cos_t_ref, sin_t_ref, cos_tab_ref, sin_tab_ref,
                     qt_ref, k_ref, vt_ref):
    xb = x_ref[...].astype(BF16)
    lat = _dot(xb, win_ref[...])
    qn = _rms(lat[:, 0:MLA_Q_RANK], gq_ref[...]).astype(BF16)
    kvn = _rms(lat[:, MLA_Q_RANK:MLA_Q_RANK + MLA_KV_RANK], gkv_ref[...]).astype(BF16)
    s0 = MLA_Q_RANK + MLA_KV_RANK
    kslab = lat[:, s0:s0 + LANES] * cos_tab_ref[...] + lat[:, s0 + LANES:s0 + 2 * LANES] * sin_tab_ref[...]
    lane = lax.broadcasted_iota(jnp.int32, (MLA_TM, LANES), 1) - MLA_QK
    chunk = (lax.broadcasted_iota(jnp.int32, (MLA_TM, LANES), 0) // CHUNK) % (ATT_T // CHUNK)
    kslab = kslab + jnp.where(lane == chunk, 1.0, 0.0)
    kmat = _dot(kvn, wk_ref[...])
    for h in range(MLA_HEADS):
        c0 = h * MLA_HEAD_PAD
        k_ref[:, c0:c0 + MLA_HEAD_PAD] = (kmat[:, c0:c0 + MLA_HEAD_PAD] + kslab).astype(BF16)
    vt_ref[0] = _dot_nt(wvt_ref[...], kvn).astype(BF16)
    qt = _dot_nt(wqt_ref[...], qn) * ATT_Q_SCALE
    cos_t = cos_t_ref[0]
    sin_t = sin_t_ref[0]
    for h in range(MLA_HEADS):
        r0 = h * MLA_HEAD_PAD
        qt_ref[0, r0:r0 + MLA_NOPE, :] = qt[r0:r0 + MLA_NOPE, :].astype(BF16)
        t1 = qt[r0 + MLA_NOPE:r0 + MLA_NOPE + MLA_HALF, :]
        t2 = qt[r0 + MLA_NOPE + MLA_HALF:r0 + MLA_QK, :]
        qt_ref[0, r0 + MLA_NOPE:r0 + MLA_NOPE + MLA_HALF, :] = (t1 * cos_t - t2 * sin_t).astype(BF16)
        qt_ref[0, r0 + MLA_NOPE + MLA_HALF:r0 + MLA_QK, :] = (t2 * cos_t + t1 * sin_t).astype(BF16)
        qt_ref[0, r0 + MLA_QK:r0 + MLA_HEAD_PAD, :] = jnp.zeros((MLA_HEAD_PAD - MLA_QK, MLA_TM), BF16)


def _mla_proj(x, w_in_pad, gq, gkv, wqt, wk, wvt, cos_t, sin_t, cos_tab, sin_tab):
    tm = MLA_TM
    nl = SEQ // tm
    hp = MLA_HEADS * MLA_HEAD_PAD
    return pl.pallas_call(
        _mla_proj_kernel,
        grid=(BATCH, nl),
        in_specs=[
            pl.BlockSpec((tm, D_MODEL), lambda b, j: (b * nl + j, 0)),
            _const_spec((D_MODEL, MLA_LAT)),
            _const_spec((1, MLA_Q_RANK)),
            _const_spec((1, MLA_KV_RANK)),
            _const_spec((hp, MLA_Q_RANK)),
            _const_spec((MLA_KV_RANK, hp)),
            _const_spec((MLA_HEADS * MLA_V, MLA_KV_RANK)),
            pl.BlockSpec((1, MLA_HALF, tm), lambda b, j: (b, 0, j)),
            pl.BlockSpec((1, MLA_HALF, tm), lambda b, j: (b, 0, j)),
            pl.BlockSpec((tm, LANES), lambda b, j: (b * nl + j, 0)),
            pl.BlockSpec((tm, LANES), lambda b, j: (b * nl + j, 0)),
        ],
        out_specs=[
            pl.BlockSpec((1, hp, tm), lambda b, j: (b, 0, j)),
            pl.BlockSpec((tm, hp), lambda b, j: (b * nl + j, 0)),
            pl.BlockSpec((1, MLA_HEADS * MLA_V, tm), lambda b, j: (b, 0, j)),
        ],
        out_shape=[
            jax.ShapeDtypeStruct((BATCH, hp, SEQ), BF16),
            jax.ShapeDtypeStruct((TOKENS, hp), BF16),
            jax.ShapeDtypeStruct((BATCH, MLA_HEADS * MLA_V, SEQ), BF16),
        ],
        compiler_params=_params(("arbitrary", "arbitrary")),
        name="mla_proj",
    )(x, w_in_pad, gq, gkv, wqt, wk, wvt, cos_t, sin_t, cos_tab, sin_tab)


def _attn_kernel(qt_ref, k_ref, vt_ref, o_ref, q2_ref, s_ref, p_ref, acc_ref):
    t_ = ATT_T
    nh = ATT_HEADS_PER_STEP
    hp = MLA_HEAD_PAD

    q2_ref[0] = qt_ref[0]
    q2_ref[1] = qt_ref[0]
    crow = lax.broadcasted_iota(jnp.int32, (ATT_MASK_ROWS, t_), 0)
    qchunk = lax.broadcasted_iota(jnp.int32, (ATT_MASK_ROWS, t_), 1) // CHUNK
    mask_rows = jnp.where((crow < t_ // CHUNK) & (crow > qchunk), NEG_BIG, 0.0).astype(BF16)
    for hh in range(nh):
        r0 = hh * hp + MLA_QK
        for qq in range(ATT_NQ):
            q2_ref[1, r0:r0 + ATT_MASK_ROWS, qq * t_:(qq + 1) * t_] = mask_rows
    p_ref[...] = jnp.zeros(p_ref.shape, BF16)
    acc_ref[...] = jnp.zeros(acc_ref.shape, F32)

    def nxt(qi, j):
        last = j >= qi
        return jnp.where(last, qi + 1, qi), jnp.where(last, 0, j + 1)

    def qk(slot, qi, j):
        diag = (j == qi).astype(jnp.int32)
        k0 = pl.multiple_of(j * t_, t_)
        q0 = pl.multiple_of(qi * t_, t_)
        for hh in range(nh):
            kt = k_ref[pl.ds(k0, t_), hh * hp:(hh + 1) * hp]
            qv = q2_ref[diag, hh * hp:(hh + 1) * hp, pl.ds(q0, t_)]
            s_ref[slot, hh] = _dot(kt, qv)

    def sm(slot, j, m, l):
        fresh = j == 0
        m_out, l_out, a_out = [], [], []
        for hh in range(nh):
            s = s_ref[slot, hh]
            m_old = jnp.where(fresh, NEG_BIG, m[hh])
            l_old = jnp.where(fresh, 0.0, l[hh])
            m_new = jnp.maximum(m_old, jnp.max(s, axis=0, keepdims=True))
            alpha = jnp.exp2(m_old - m_new)
            p = jnp.exp2(s - m_new)
            l_out.append(alpha * l_old + jnp.sum(p, axis=0, keepdims=True))
            m_out.append(m_new)
            a_out.append(alpha)
            p_ref[slot, hh] = p.astype(BF16)
        return tuple(m_out), tuple(l_out), tuple(a_out)

    def pv(slot, qi, j, alpha, l):
        k0 = pl.multiple_of(j * t_, t_)
        q0 = pl.multiple_of(qi * t_, t_)
        for hh in range(nh):
            vt = vt_ref[0, hh * MLA_V:(hh + 1) * MLA_V, pl.ds(k0, t_)]
            upd = _dot(vt, p_ref[slot, hh])
            acc_old = jnp.where(j == 0, 0.0, acc_ref[hh])
            acc_new = acc_old * alpha[hh] + upd
            acc_ref[hh] = acc_new
            o_ref[0, hh * MLA_V:(hh + 1) * MLA_V, pl.ds(q0, t_)] = (acc_new * (1.0 / l[hh])).astype(BF16)

    def step(par, carry, do_qk=True, do_sm=True):
        qa, ja, qb, jb, qc, jc, m, l, al = carry
        pv(par, qc, jc, al, l)
        if do_sm:
            m2, l2, al2 = sm(1 - par, jb, m, l)
        else:
            m2, l2, al2 = m, l, al
        if do_qk:
            qk(par, qa, ja)
        qa2, ja2 = nxt(qa, ja)
        return (qa2, ja2, qa, ja, qb, jb, m2, l2, al2)

    zero = jnp.int32(0)
    qk(0, zero, zero)
    row = lambda v: tuple(jnp.full((1, t_), v, F32) for _ in range(nh))
    carry = (jnp.int32(1), zero, zero, zero, zero, zero, row(NEG_BIG), row(1.0), row(1.0))
    n_items = ATT_NQ * (ATT_NQ + 1) // 2
    assert n_items % 2 == 0
    carry = lax.fori_loop(0, (n_items - 2) // 2, lambda u, cr: step(0, step(1, cr)), carry)
    carry = step(1, carry)
    carry = step(0, carry, do_qk=False)
    step(1, carry, do_qk=False, do_sm=False)


def _attention(qt, k, vt):
    nh = ATT_HEADS_PER_STEP
    return pl.pallas_call(
        _attn_kernel,
        grid=(BATCH, MLA_HEADS // nh),
        in_specs=[
            pl.BlockSpec((1, nh * MLA_HEAD_PAD, SEQ), lambda b, h: (b, h, 0)),
            pl.BlockSpec((SEQ, nh * MLA_HEAD_PAD), lambda b, h: (b, h)),
            pl.BlockSpec((1, nh * MLA_V, SEQ), lambda b, h: (b, h, 0)),
        ],
        out_specs=pl.BlockSpec((1, nh * MLA_V, SEQ), lambda b, h: (b, h, 0)),
        out_shape=jax.ShapeDtypeStruct((BATCH, MLA_HEADS * MLA_V, SEQ), BF16),
        scratch_shapes=[
            pltpu.VMEM((2, nh * MLA_HEAD_PAD, SEQ), BF16),
            pltpu.VMEM((2, nh, ATT_T, ATT_T), F32),
            pltpu.VMEM((2, nh, ATT_T, ATT_T), BF16),
            pltpu.VMEM((nh, MLA_V, ATT_T), F32),
        ],
        compiler_params=_params(("arbitrary", "arbitrary")),
        name="mla_attention",
    )(qt, k, vt)


PROJ_TM = 512


def _proj_ln_kernel(x_ref, at_ref, w_ref, g_ref, b_ref, o_ref):
    y = lax.dot_general(at_ref[0], w_ref[...], (((0,), (0,)), ((), ())), preferred_element_type=F32)
    z = DN_ALPHA * x_ref[...] + y
    o_ref[...] = _layer_norm(z, g_ref[...], b_ref[...])


def _proj_ln(x, a_t, w, g, b):
    tm = PROJ_TM
    nl = SEQ // tm
    kdim = a_t.shape[1]
    return pl.pallas_call(
        _proj_ln_kernel,
        grid=(BATCH, nl),
        in_specs=[
            pl.BlockSpec((tm, D_MODEL), lambda bi, j: (bi * nl + j, 0)),
            pl.BlockSpec((1, kdim, tm), lambda bi, j: (bi, 0, j)),
            _const_spec((kdim, D_MODEL)),
            _const_spec((1, D_MODEL)),
            _const_spec((1, D_MODEL)),
        ],
        out_specs=pl.BlockSpec((tm, D_MODEL), lambda bi, j: (bi * nl + j, 0)),
        out_shape=jax.ShapeDtypeStruct((TOKENS, D_MODEL), F32),
        compiler_params=_params(("arbitrary", "arbitrary")),
        name="proj_ln",
    )(x, a_t, w, g, b)


SSM_TM = 512
SSD_Q = 128
SSD_TM = 512
LOG2E = math.log2(math.e)


def _ssm_in_kernel(x_ref, wz_ref, wxbc_ref, wdt_ref, cw_ref, cb_ref, dtb_ref,
                   z_ref, xs_ref, b_ref, bt_ref, c_ref, dt_ref, dtt_ref, buf_ref, halo_ref):
    tm = SSM_TM
    blk = CONV_BLK
    nblk = SSM_CONV_DIM // blk
    nz = SSM_D_INNER // blk

    @pl.when(pl.program_id(1) == 0)
    def _():
        halo_ref[...] = jnp.zeros(halo_ref.shape, F32)

    xb = x_ref[...].astype(BF16)
    dt = jax.nn.softplus(_dot(xb, wdt_ref[...]) + dtb_ref[...])
    dt_ref[...] = dt
    dtt_ref[0] = dt.T[0:SSM_HEADS, :]

    def project(k):
        buf_ref[k % 2, HALO:HALO + tm, :] = _dot(xb, wxbc_ref[:, k * blk:(k + 1) * blk])

    def gate_proj(k):
        z_ref[:, k * blk:(k + 1) * blk] = _dot(xb, wz_ref[:, k * blk:(k + 1) * blk]).astype(BF16)

    def conv(k):
        c0 = k * blk
        v = _silu(_conv_block(buf_ref.at[k % 2], halo_ref, cw_ref, cb_ref, c0, SSM_CONV, tm))
        if c0 < SSM_D_INNER:
            xs_ref[:, c0:c0 + blk] = v.astype(BF16)
        elif c0 < SSM_D_INNER + SSM_BC:
            n0 = c0 - SSM_D_INNER
            b_ref[:, n0:n0 + blk] = v.astype(BF16)
            bt_ref[0, n0:n0 + blk, :] = v.T.astype(BF16)
        else:
            n0 = c0 - SSM_D_INNER - SSM_BC
            c_ref[:, n0:n0 + blk] = v.astype(BF16)

    project(0)
    for k in range(nblk):
        if k + 1 < nblk:
            project(k + 1)
        if k % 2 == 0 and k // 2 < nz:
            gate_proj(k // 2)
        conv(k)


def _ssm_in(x, w_z, w_xbc, w_dt, conv_w, conv_b, dt_bias):
    tm = SSM_TM
    nl = SEQ // tm
    tok = lambda n: pl.BlockSpec((tm, n), lambda b, j: (b * nl + j, 0))
    return pl.pallas_call(
        _ssm_in_kernel,
        grid=(BATCH, nl),
        in_specs=[
            tok(D_MODEL),
            _const_spec((D_MODEL, SSM_D_INNER)),
            _const_spec((D_MODEL, SSM_CONV_DIM)),
            _const_spec((D_MODEL, LANES)),
            _const_spec((SSM_CONV, SSM_CONV_DIM)),
            _const_spec((1, SSM_CONV_DIM)),
            _const_spec((1, LANES)),
        ],
        out_specs=[
            tok(SSM_D_INNER),
            tok(SSM_D_INNER),
            tok(SSM_BC),
            pl.BlockSpec((1, SSM_BC, tm), lambda b, j: (b, 0, j)),
            tok(SSM_BC),
            tok(LANES),
            pl.BlockSpec((1, SSM_HEADS, tm), lambda b, j: (b, 0, j)),
        ],
        out_shape=[
            jax.ShapeDtypeStruct((TOKENS, SSM_D_INNER), BF16),
            jax.ShapeDtypeStruct((TOKENS, SSM_D_INNER), BF16),
            jax.ShapeDtypeStruct((TOKENS, SSM_BC), BF16),
            jax.ShapeDtypeStruct((BATCH, SSM_BC, SEQ), BF16),
            jax.ShapeDtypeStruct((TOKENS, SSM_BC), BF16),
            jax.ShapeDtypeStruct((TOKENS, LANES), F32),
            jax.ShapeDtypeStruct((BATCH, SSM_HEADS, SEQ), F32),
        ],
        scratch_shapes=[
            pltpu.VMEM((2, tm + HALO, CONV_BLK), F32),
            pltpu.VMEM((HALO, SSM_CONV_DIM), F32),
        ],
        compiler_params=_params(("arbitrary", "arbitrary")),
        name="ssm_in",
    )(x, w_z, w_xbc, w_dt, conv_w, conv_b, dt_bias)


def _ssd_kernel(x_ref, z_ref, xs_ref, b_ref, bt_ref, c_ref, dt_ref, dtt_ref,
                alog_row_ref, alog_col_ref, dskip_ref, ng_ref, rexp_ref, wout_ref, g_ref, bb_ref,
                o_ref, state_ref, y_ref, xbd_ref, yb_ref):
    q = SSD_Q
    hi = lax.Precision.HIGHEST
    hw = SSM_HPG * SSM_HEAD_DIM

    @pl.when(pl.program_id(1) == 0)
    def _():
        state_ref[...] = jnp.zeros(state_ref.shape, F32)

    xbd_ref[...] = jnp.zeros(xbd_ref.shape, BF16)

    row = lax.broadcasted_iota(jnp.int32, (q, q), 0)
    col = lax.broadcasted_iota(jnp.int32, (q, q), 1)
    lower = col <= row
    tri = lower.astype(F32)
    a_row = -jnp.exp(alog_row_ref[...]) * LOG2E
    a_col = -jnp.exp(alog_col_ref[...]) * LOG2E
    lane_head = lax.broadcasted_iota(jnp.int32, (1, hw), 1) // SSM_HEAD_DIM

    def chunk(s, carry):
        r0 = pl.multiple_of(s * q, q)
        rows = pl.ds(r0, q)
        dt = dt_ref[rows, :]
        dt_t = dtt_ref[0, :, rows]
        acum = jnp.dot(tri, dt * a_row, precision=hi, preferred_element_type=F32)
        acum_t = lax.dot_general(dt_t * a_col, tri, (((1,), (1,)), ((), ())), precision=hi,
                                 preferred_element_type=F32)
        e_col = jnp.exp2(acum).astype(BF16)
        w_col = (jnp.exp2(acum[q - 1:q, :] - acum) * dt).astype(BF16)
        rowterm = acum_t - jnp.log2(dt_t)
        cd = jnp.exp2(acum_t[:, q - 1:q])
        for grp in range(SSM_GROUPS):
            n0 = grp * SSM_STATE
            c0 = grp * hw
            cg = c_ref[rows, n0:n0 + SSM_STATE]
            cb = _dot_nt(cg, b_ref[rows, n0:n0 + SSM_STATE])
            rg = rexp_ref[:, c0:c0 + hw]
            eg = _dot(e_col, rg)
            wg = _dot(w_col, rg)
            ms = []
            cdg = jnp.zeros((1, hw), F32)
            for r in range(SSM_HPG):
                h = grp * SSM_HPG + r
                p0 = c0 + r * SSM_HEAD_DIM
                seg = acum[:, h:h + 1] - rowterm[h:h + 1, :]
                ms.append((cb * jnp.exp2(jnp.where(lower, seg, NEG_BIG))).astype(BF16))
                xbd_ref[grp, r * q:(r + 1) * q, r * SSM_HEAD_DIM:(r + 1) * SSM_HEAD_DIM] = xs_ref[rows, p0:p0 + SSM_HEAD_DIM]
                cdg = jnp.where(lane_head == r, cd[h:h + 1, :], cdg)
            s_old = state_ref[grp]
            xgf = xs_ref[rows, c0:c0 + hw].astype(F32)
            yg = (_dot(jnp.concatenate(ms, axis=1), xbd_ref[grp])
                  + _dot(cg, s_old.astype(BF16)) * eg + dskip_ref[:, c0:c0 + hw] * xgf)
            y_ref[rows, c0:c0 + hw] = yg
            state_ref[grp] = s_old * cdg + _dot(bt_ref[0, n0:n0 + SSM_STATE, rows], (xgf * wg).astype(BF16))
        return carry

    lax.fori_loop(0, SSD_TM // q, chunk, 0)

    for blk in range(SSD_TM // q):
        rows = slice(blk * q, (blk + 1) * q)
        y = y_ref[rows, :] * _silu(z_ref[rows, :].astype(F32))
        for grp in range(SSM_GROUPS):
            c0 = grp * hw
            yb_ref[rows, c0:c0 + hw] = _rms(y[:, c0:c0 + hw], ng_ref[:, c0:c0 + hw]).astype(BF16)
    zres = DN_ALPHA * x_ref[...] + _dot(yb_ref[...], wout_ref[...])
    o_ref[...] = _layer_norm(zres, g_ref[...], bb_ref[...])


def _ssd(x, z, xs, bm, bt, cm, dt, dtt, alog_row, alog_col, dskip, ng, rexp, w_out, g, b):
    tm = SSD_TM
    nl = SEQ // tm
    tok = lambda n: pl.BlockSpec((tm, n), lambda bi, j: (bi * nl + j, 0))
    return pl.pallas_call(
        _ssd_kernel,
        grid=(BATCH, nl),
        in_specs=[
            tok(D_MODEL),
            tok(SSM_D_INNER),
            tok(SSM_D_INNER),
            tok(SSM_BC),
            pl.BlockSpec((1, SSM_BC, tm), lambda bi, j: (bi, 0, j)),
            tok(SSM_BC),
            tok(LANES),
            pl.BlockSpec((1, SSM_HEADS, tm), lambda bi, j: (bi, 0, j)),
            _const_spec((1, LANES)),
            _const_spec((SSM_HEADS, 1)),
            _const_spec((1, SSM_D_INNER)),
            _const_spec((1, SSM_D_INNER)),
            _const_spec((LANES, SSM_D_INNER)),
            _const_spec((SSM_D_INNER, D_MODEL)),
            _const_spec((1, D_MODEL)),
            _const_spec((1, D_MODEL)),
        ],
        out_specs=tok(D_MODEL),
        out_shape=jax.ShapeDtypeStruct((TOKENS, D_MODEL), F32),
        scratch_shapes=[
            pltpu.VMEM((SSM_GROUPS, SSM_STATE, SSM_HPG * SSM_HEAD_DIM), F32),
            pltpu.VMEM((tm, SSM_D_INNER), F32),
            pltpu.VMEM((SSM_GROUPS, SSM_HPG * SSD_Q, SSM_HPG * SSM_HEAD_DIM), BF16),
            pltpu.VMEM((tm, SSM_D_INNER), BF16),
        ],
        compiler_params=_params(("arbitrary", "arbitrary")),
        name="ssd_scan",
    )(x, z, xs, bm, bt, cm, dt, dtt, alog_row, alog_col, dskip, ng, rexp, w_out, g, b)


def _row(v):
    return v.reshape(1, -1).astype(F32)


def _mamba_layer(x, w_in, conv_w, conv_b, dt_bias, a_log, d_skip, norm_g, w_out, g, b):
    w_z = w_in[:, :SSM_D_INNER].astype(BF16)
    w_xbc = w_in[:, SSM_D_INNER:SSM_D_INNER + SSM_CONV_DIM].astype(BF16)
    w_dt = jnp.pad(w_in[:, SSM_D_INNER + SSM_CONV_DIM:], ((0, 0), (0, LANES - SSM_HEADS))).astype(BF16)
    dtb = jnp.pad(dt_bias, (0, LANES - SSM_HEADS)).reshape(1, LANES)
    z, xs, bm, bt, cm, dt, dtt = _ssm_in(x, w_z, w_xbc, w_dt, conv_w, _row(conv_b), dtb)
    alog_row = jnp.pad(a_log, (0, LANES - SSM_HEADS)).reshape(1, LANES)
    alog_col = a_log.reshape(SSM_HEADS, 1)
    dskip = jnp.repeat(d_skip, SSM_HEAD_DIM).reshape(1, SSM_D_INNER)
    rexp = jnp.repeat(jnp.eye(LANES, SSM_HEADS, dtype=F32), SSM_HEAD_DIM, axis=1).astype(BF16)
    return _ssd(x, z, xs, bm, bt, cm, dt, dtt, alog_row, alog_col, dskip, _row(norm_g), rexp,
                w_out.astype(BF16), _row(g), _row(b))


def _sg_layer(x, w_in, b_in, ln_g, ln_b, w_s, b_s, w_out, g, b):
    b_s_wide = jnp.repeat(b_s.T, SG_GROUP_DIM, axis=1)
    return _spatial_gating(x, w_in.astype(BF16), _row(b_in), _row(ln_g), _row(ln_b), w_s, b_s_wide,
                           w_out.astype(BF16), _row(g), _row(b))


def _mla_layer(x, positions, w_in, q_norm_g, w_q_b, kv_norm_g, w_kv_b, w_out, g, b):
    h = MLA_HEADS
    s0 = MLA_Q_RANK + MLA_KV_RANK
    t1 = w_in[:, s0:s0 + MLA_HALF]
    t2 = w_in[:, s0 + MLA_HALF:s0 + MLA_ROPE]
    zl = jnp.zeros((D_MODEL, MLA_NOPE), F32)
    zh = jnp.zeros((D_MODEL, LANES - MLA_QK), F32)
    w_in_pad = jnp.concatenate([w_in[:, :s0], zl, t1, t2, zh, zl, -t2, t1, zh], axis=1).astype(BF16)

    wq = w_q_b.reshape(MLA_Q_RANK, h, MLA_QK)
    wq = jnp.pad(wq, ((0, 0), (0, 0), (0, MLA_HEAD_PAD - MLA_QK)))
    wqt = wq.reshape(MLA_Q_RANK, h * MLA_HEAD_PAD).T.astype(BF16)
    wkv = w_kv_b.reshape(MLA_KV_RANK, h, MLA_NOPE + MLA_V)
    wk = jnp.pad(wkv[:, :, :MLA_NOPE], ((0, 0), (0, 0), (0, MLA_HEAD_PAD - MLA_NOPE)))
    wk = wk.reshape(MLA_KV_RANK, h * MLA_HEAD_PAD).astype(BF16)
    wvt = wkv[:, :, MLA_NOPE:].reshape(MLA_KV_RANK, h * MLA_V).T.astype(BF16)

    inv = (ROPE_THETA ** (-(jnp.arange(MLA_HALF, dtype=F32) * 2.0 / MLA_ROPE))).reshape(MLA_HALF, 1)
    cos_t, sin_t, cos_tab, sin_tab = _rope_tables(positions, inv)
    qt, k, vt = _mla_proj(x, w_in_pad, _row(q_norm_g), _row(kv_norm_g), wqt, wk, wvt,
                          cos_t, sin_t, cos_tab, sin_tab)
    o = _attention(qt, k, vt)
    return _proj_ln(x, o, w_out.astype(BF16), _row(g), _row(b))


def kernel(x, positions, ssm_w_in, ssm_conv_w, ssm_conv_b, ssm_dt_bias, ssm_a_log, ssm_d, ssm_norm_g, ssm_w_out, sg_w_in, sg_b_in, sg_ln_g, sg_ln_b, sg_w_s, sg_b_s, sg_w_out, mla_w_in, mla_q_norm_g, mla_w_q_b, mla_kv_norm_g, mla_w_kv_b, mla_w_out, ffn_w_in, ffn_conv_w, ffn_conv_b, ffn_w_out, ln_g, ln_b):
    h = x.reshape(TOKENS, D_MODEL)
    for i in range(DEPTH):
        m, j = i % N_MIXERS, i // N_MIXERS
        g0, b0 = ln_g[i, 0], ln_b[i, 0]
        if m == 0:
            h = _mamba_layer(h, ssm_w_in[j], ssm_conv_w[j], ssm_conv_b[j], ssm_dt_bias[j], ssm_a_log[j],
                             ssm_d[j], ssm_norm_g[j], ssm_w_out[j], g0, b0)
        elif m == 1:
            h = _sg_layer(h, sg_w_in[j], sg_b_in[j], sg_ln_g[j], sg_ln_b[j], sg_w_s[j], sg_b_s[j],
                          sg_w_out[j], g0, b0)
        else:
            h = _mla_layer(h, positions, mla_w_in[j], mla_q_norm_g[j], mla_w_q_b[j], mla_kv_norm_g[j],
                           mla_w_kv_b[j], mla_w_out[j], g0, b0)
        h = _ffn(h, ffn_w_in[i].astype(BF16), ffn_conv_w[i], _row(ffn_conv_b[i]),
                 ffn_w_out[i].astype(BF16), _row(ln_g[i, 1]), _row(ln_b[i, 1]))
    return h.reshape(BATCH, SEQ, D_MODEL)
```

```python
import functools
import math

import jax
import jax.numpy as jnp
from jax import lax
from jax.experimental import pallas as pl
from jax.experimental.pallas import tpu as pltpu

D_MODEL = 1024
BATCH = 8
SEQ = 4096
DEPTH = 4
TOKENS = BATCH * SEQ

CHUNK = 64
N_MIXERS = 3

DN_ALPHA = (2.0 * DEPTH) ** 0.25
LN_EPS = 1e-5
RMS_EPS = 1e-6

SSM_D_INNER = 2048
SSM_HEAD_DIM = 64
SSM_HEADS = 32
SSM_GROUPS = 8
SSM_HPG = 4
SSM_STATE = 128
SSM_CONV = 4
SSM_CONV_DIM = 4096
SSM_BC = SSM_GROUPS * SSM_STATE

SG_BLOCK = 128
SG_WIDTH = 2048
SG_GROUPS = 8
SG_GROUP_DIM = 256

MLA_HEADS = 16
MLA_Q_RANK = 384
MLA_KV_RANK = 256
MLA_NOPE = 64
MLA_ROPE = 32
MLA_HALF = 16
MLA_V = 64
MLA_QK = MLA_NOPE + MLA_ROPE
ROPE_THETA = 10000.0

FFN_HIDDEN = 2816
FFN_CONV = 3

LANES = 128
SUBLANES = 8
PERM_BLOCK = 128
PERM_GROUPS = PERM_BLOCK // SUBLANES
VMEM_LIMIT = 56 * 1024 * 1024

BF16 = jnp.bfloat16
F32 = jnp.float32
NEG_BIG = -1e30


def _dot(a, b):
    return jnp.dot(a, b, preferred_element_type=F32)


def _dot_nt(a, b):
    return lax.dot_general(a, b, (((1,), (1,)), ((), ())), preferred_element_type=F32)


def _layer_norm(z, g, b):
    mu = jnp.mean(z, axis=-1, keepdims=True)
    zc = z - mu
    var = jnp.mean(zc * zc, axis=-1, keepdims=True)
    return zc * lax.rsqrt(var + LN_EPS) * g + b


def _silu(x):
    return x * (1.0 / (1.0 + jnp.exp(-x)))


def _const_spec(shape):
    nd = len(shape)
    return pl.BlockSpec(shape, lambda *_: (0,) * nd, pipeline_mode=pl.Buffered(1))


def _params(sem):
    return pltpu.CompilerParams(dimension_semantics=sem, vmem_limit_bytes=VMEM_LIMIT)


FFN_TM = 512
CONV_BLK = 2 * LANES
FFN_NBLK = FFN_HIDDEN // CONV_BLK
FFN_OUT_GROUP = 3
FFN_NGROUP = -(-FFN_NBLK // FFN_OUT_GROUP)


def _permute_tokens(a):
    shp = a.shape
    a = a.reshape(shp[0], shp[1] // PERM_BLOCK, SUBLANES, PERM_GROUPS, *shp[2:])
    return jnp.swapaxes(a, 2, 3).reshape(shp)


def _unpermute_tokens(a):
    shp = a.shape
    a = a.reshape(shp[0], shp[1] // PERM_BLOCK, PERM_GROUPS, SUBLANES, *shp[2:])
    return jnp.swapaxes(a, 2, 3).reshape(shp)


def _perm_token(idx):
    return (idx % SUBLANES) * PERM_GROUPS + (idx % PERM_BLOCK) // SUBLANES


def _conv_rows(taps, tm):
    return (tm // PERM_BLOCK) * (PERM_BLOCK + (taps - 1) * SUBLANES)


def _conv_store(buf, res, taps, tm):
    pad = (taps - 1) * SUBLANES
    stride = PERM_BLOCK + pad
    for blk in range(tm // PERM_BLOCK):
        buf[blk * stride + pad:(blk + 1) * stride, :] = res[blk * PERM_BLOCK:(blk + 1) * PERM_BLOCK, :]


def _conv_block(buf, halo_ref, cw_ref, cb_ref, c0, taps, tm):
    pad = (taps - 1) * SUBLANES
    stride = PERM_BLOCK + pad
    nb = tm // PERM_BLOCK
    top = lax.broadcasted_iota(jnp.int32, (SUBLANES, CONV_BLK), 0) == 0
    outs = []
    for blk in range(nb):
        base = blk * stride
        for s in range(1, taps):
            g0 = base + stride - s * SUBLANES
            cur = buf[g0:g0 + SUBLANES, :]
            if blk == 0:
                prev = halo_ref[(s - 1) * SUBLANES:s * SUBLANES, c0:c0 + CONV_BLK]
            else:
                prev = buf[base - s * SUBLANES:base - (s - 1) * SUBLANES, :]
            v0 = base + pad - s * SUBLANES
            buf[v0:v0 + SUBLANES, :] = jnp.where(top, pltpu.roll(prev, 1, 0), pltpu.roll(cur, 1, 0))
        acc = cb_ref[:, c0:c0 + CONV_BLK]
        for t in range(taps):
            r0 = base + t * SUBLANES
            acc = acc + cw_ref[t:t + 1, c0:c0 + CONV_BLK] * buf[r0:r0 + PERM_BLOCK, :]
        outs.append(acc)
    for s in range(1, taps):
        g0 = nb * stride - s * SUBLANES
        halo_ref[(s - 1) * SUBLANES:s * SUBLANES, c0:c0 + CONV_BLK] = buf[g0:g0 + SUBLANES, :]
    return outs


def _ffn_kernel(x_ref, win_ref, cw_ref, cb_ref, wout_ref, g_ref, b_ref, o_ref,
                buf_ref, halo_ref, act_ref, acc_ref, xb_ref):
    tm = FFN_TM

    @pl.when(pl.program_id(0) % (SEQ // tm) == 0)
    def _():
        halo_ref[...] = jnp.zeros(halo_ref.shape, F32)

    xb_ref[...] = x_ref[...].astype(BF16)

    def project(k):
        for half in range(2):
            off = half * FFN_HIDDEN + k * CONV_BLK
            _conv_store(buf_ref.at[k % 2, half], _dot(xb_ref[...], win_ref[:, off:off + CONV_BLK]), FFN_CONV, tm)

    def gate(k):
        conv = [_conv_block(buf_ref.at[k % 2, half], halo_ref, cw_ref, cb_ref,
                            half * FFN_HIDDEN + k * CONV_BLK, FFN_CONV, tm) for half in range(2)]
        j = k % FFN_OUT_GROUP
        for blk in range(tm // PERM_BLOCK):
            act_ref[k // FFN_OUT_GROUP, blk * PERM_BLOCK:(blk + 1) * PERM_BLOCK, j * CONV_BLK:(j + 1) * CONV_BLK] = (
                _silu(conv[0][blk]) * conv[1][blk]).astype(BF16)

    def contract(grp):
        k0 = grp * FFN_OUT_GROUP
        n = min(FFN_OUT_GROUP, FFN_NBLK - k0) * CONV_BLK
        part = _dot(act_ref[grp, :, 0:n], wout_ref[k0 * CONV_BLK:k0 * CONV_BLK + n, :])
        if grp == 0:
            acc_ref[...] = part
        else:
            acc_ref[...] += part

    project(0)
    for k in range(FFN_NBLK):
        if k + 1 < FFN_NBLK:
            project(k + 1)
        gate(k)
        if (k + 1) % FFN_OUT_GROUP == 0 or k + 1 == FFN_NBLK:
            contract(k // FFN_OUT_GROUP)
    z = DN_ALPHA * x_ref[...] + acc_ref[...]
    o_ref[...] = _layer_norm(z, g_ref[...], b_ref[...])


def _ffn(x, w_in, conv_w, conv_b, w_out, g, b):
    tm = FFN_TM
    return pl.pallas_call(
        _ffn_kernel,
        grid=(TOKENS // tm,),
        in_specs=[
            pl.BlockSpec((tm, D_MODEL), lambda i: (i, 0)),
            _const_spec((D_MODEL, 2 * FFN_HIDDEN)),
            _const_spec((FFN_CONV, 2 * FFN_HIDDEN)),
            _const_spec((1, 2 * FFN_HIDDEN)),
            _const_spec((FFN_HIDDEN, D_MODEL)),
            _const_spec((1, D_MODEL)),
            _const_spec((1, D_MODEL)),
        ],
        out_specs=pl.BlockSpec((tm, D_MODEL), lambda i: (i, 0)),
        out_shape=jax.ShapeDtypeStruct((TOKENS, D_MODEL), F32),
        scratch_shapes=[
            pltpu.VMEM((2, 2, _conv_rows(FFN_CONV, tm), CONV_BLK), F32),
            pltpu.VMEM(((FFN_CONV - 1) * SUBLANES, 2 * FFN_HIDDEN), F32),
            pltpu.VMEM((FFN_NGROUP, tm, FFN_OUT_GROUP * CONV_BLK), BF16),
            pltpu.VMEM((tm, D_MODEL), F32),
            pltpu.VMEM((tm, D_MODEL), BF16),
        ],
        compiler_params=_params(("arbitrary",)),
        name="conv_ffn",
    )(x, w_in, conv_w, conv_b, w_out, g, b)


SG_TM = 512
SG_PROJ_BLK = 4 * LANES


def _sg_kernel(x_ref, win_ref, bin_ref, lng_ref, lnb_ref, ws_ref, bs_ref, wout_ref, g_ref, b_ref,
               o_ref, v32_ref, v_ref, gated_ref, h_ref, xb_ref, acc_ref):
    tm = SG_TM
    xb_ref[...] = x_ref[...].astype(BF16)
    nhalf = SG_WIDTH // SG_PROJ_BLK
    gpb = SG_PROJ_BLK // SG_GROUP_DIM

    def project(k):
        c0 = k * SG_PROJ_BLK
        h_ref[k % 2] = _dot(xb_ref[...], win_ref[:, c0:c0 + SG_PROJ_BLK]) + bin_ref[:, c0:c0 + SG_PROJ_BLK]

    project(nhalf)
    for k in range(nhalf, 2 * nhalf):
        project(k + 1 if k + 1 < 2 * nhalf else 0)
        c0 = (k - nhalf) * SG_PROJ_BLK
        v32_ref[:, c0:c0 + SG_PROJ_BLK] = jax.nn.gelu(h_ref[k % 2])
    v_ref[...] = _layer_norm(v32_ref[...], lng_ref[...], lnb_ref[...]).astype(BF16)

    row = lax.broadcasted_iota(jnp.int32, (SG_BLOCK, SG_BLOCK), 0)
    col = lax.broadcasted_iota(jnp.int32, (SG_BLOCK, SG_BLOCK), 1)
    causal = _perm_token(col) <= _perm_token(row)
    for k in range(nhalf):
        if k + 1 < nhalf:
            project(k + 1)
        u = jax.nn.gelu(h_ref[k % 2])
        for gi in range(gpb):
            grp = k * gpb + gi
            c0 = grp * SG_GROUP_DIM
            ws = jnp.where(causal, ws_ref[grp], 0.0).astype(BF16)
            bias = bs_ref[:, c0:c0 + SG_GROUP_DIM]
            for blk in range(tm // SG_BLOCK):
                r0 = blk * SG_BLOCK
                mixed = _dot(ws, v_ref[r0:r0 + SG_BLOCK, c0:c0 + SG_GROUP_DIM]) + bias
                gated_ref[k, r0:r0 + SG_BLOCK, gi * SG_GROUP_DIM:(gi + 1) * SG_GROUP_DIM] = (
                    u[r0:r0 + SG_BLOCK, gi * SG_GROUP_DIM:(gi + 1) * SG_GROUP_DIM] * mixed).astype(BF16)
        part = _dot(gated_ref[k], wout_ref[k * SG_PROJ_BLK:(k + 1) * SG_PROJ_BLK, :])
        if k == 0:
            acc_ref[...] = part
        else:
            acc_ref[...] += part
    z = DN_ALPHA * x_ref[...] + acc_ref[...]
    o_ref[...] = _layer_norm(z, g_ref[...], b_ref[...])


def _spatial_gating(x, w_in, b_in, ln_g, ln_b, w_s, b_s_wide, w_out, g, b):
    tm = SG_TM
    return pl.pallas_call(
        _sg_kernel,
        grid=(TOKENS // tm,),
        in_specs=[
            pl.BlockSpec((tm, D_MODEL), lambda i: (i, 0)),
            _const_spec((D_MODEL, 2 * SG_WIDTH)),
            _const_spec((1, 2 * SG_WIDTH)),
            _const_spec((1, SG_WIDTH)),
            _const_spec((1, SG_WIDTH)),
            _const_spec((SG_GROUPS, SG_BLOCK, SG_BLOCK)),
            _const_spec((SG_BLOCK, SG_WIDTH)),
            _const_spec((SG_WIDTH, D_MODEL)),
            _const_spec((1, D_MODEL)),
            _const_spec((1, D_MODEL)),
        ],
        out_specs=pl.BlockSpec((tm, D_MODEL), lambda i: (i, 0)),
        out_shape=jax.ShapeDtypeStruct((TOKENS, D_MODEL), F32),
        scratch_shapes=[
            pltpu.VMEM((tm, SG_WIDTH), F32),
            pltpu.VMEM((tm, SG_WIDTH), BF16),
            pltpu.VMEM((SG_WIDTH // SG_PROJ_BLK, tm, SG_PROJ_BLK), BF16),
            pltpu.VMEM((2, tm, SG_PROJ_BLK), F32),
            pltpu.VMEM((tm, D_MODEL), BF16),
            pltpu.VMEM((tm, D_MODEL), F32),
        ],
        compiler_params=_params(("arbitrary",)),
        name="spatial_gating",
    )(x, w_in, b_in, ln_g, ln_b, w_s, b_s_wide, w_out, g, b)


ROPE_TL = 512
MLA_TM = 512
MLA_LAT = MLA_Q_RANK + MLA_KV_RANK + 2 * LANES
MLA_HEAD_PAD = LANES
ATT_T = 256
ATT_NQ = SEQ // ATT_T
ATT_HEADS_PER_STEP = 4
ATT_MASK_ROWS = 16
ATT_Q_SCALE = (MLA_QK ** -0.5) * math.log2(math.e)


def _rope_kernel(pos_ref, inv_ref, cos_t_ref, sin_t_ref, cos_tab_ref, sin_tab_ref):
    ang = inv_ref[...] * pos_ref[0].astype(F32)
    c = jnp.cos(ang)
    s = jnp.sin(ang)
    cos_t_ref[0] = c
    sin_t_ref[0] = s
    zlo = jnp.zeros((MLA_NOPE, ROPE_TL), F32)
    zhi = jnp.zeros((LANES - MLA_QK, ROPE_TL), F32)
    cos_tab_ref[...] = jnp.concatenate([zlo, c, c, zhi], axis=0).T
    sin_tab_ref[...] = jnp.concatenate([zlo, s, s, zhi], axis=0).T


def _rope_tables(positions, inv):
    nl = SEQ // ROPE_TL
    pos3 = positions.reshape(BATCH, 1, SEQ)
    return pl.pallas_call(
        _rope_kernel,
        grid=(BATCH, nl),
        in_specs=[
            pl.BlockSpec((1, 1, ROPE_TL), lambda b, j: (b, 0, j)),
            pl.BlockSpec((MLA_HALF, 1), lambda b, j: (0, 0)),
        ],
        out_specs=[
            pl.BlockSpec((1, MLA_HALF, ROPE_TL), lambda b, j: (b, 0, j)),
            pl.BlockSpec((1, MLA_HALF, ROPE_TL), lambda b, j: (b, 0, j)),
            pl.BlockSpec((ROPE_TL, LANES), lambda b, j: (b * nl + j, 0)),
            pl.BlockSpec((ROPE_TL, LANES), lambda b, j: (b * nl + j, 0)),
        ],
        out_shape=[
            jax.ShapeDtypeStruct((BATCH, MLA_HALF, SEQ), F32),
            jax.ShapeDtypeStruct((BATCH, MLA_HALF, SEQ), F32),
            jax.ShapeDtypeStruct((TOKENS, LANES), F32),
            jax.ShapeDtypeStruct((TOKENS, LANES), F32),
        ],
        compiler_params=_params(("arbitrary", "arbitrary")),
        name="rope_tables",
    )(pos3, inv)


def _rms(x, g):
    return x * lax.rsqrt(jnp.mean(x * x, axis=-1, keepdims=True) + RMS_EPS) * g


def _mla_proj_kernel(x_ref, win_ref, gq_ref, gkv_ref, wqt_ref, wk_ref, wvt_ref,
                     cos_t_ref, sin_t_ref, cos_tab_ref, sin_tab_ref,
                     qt_ref, k_ref, vt_ref):
    xb = x_ref[...].astype(BF16)
    lat = _dot(xb, win_ref[...])
    qn = _rms(lat[:, 0:MLA_Q_RANK], gq_ref[...]).astype(BF16)
    kvn = _rms(lat[:, MLA_Q_RANK:MLA_Q_RANK + MLA_KV_RANK], gkv_ref[...]).astype(BF16)
    s0 = MLA_Q_RANK + MLA_KV_RANK
    kslab = lat[:, s0:s0 + LANES] * cos_tab_ref[...] + lat[:, s0 + LANES:s0 + 2 * LANES] * sin_tab_ref[...]
    lane = lax.broadcasted_iota(jnp.int32, (MLA_TM, LANES), 1) - MLA_QK
    rowi = lax.broadcasted_iota(jnp.int32, (MLA_TM, LANES), 0)
    chunk = ((rowi % ATT_T) // PERM_BLOCK * PERM_BLOCK + _perm_token(rowi)) // CHUNK
    kslab = kslab + jnp.where(lane == chunk, 1.0, 0.0)
    kmat = _dot(kvn, wk_ref[...])
    for h in range(MLA_HEADS):
        c0 = h * MLA_HEAD_PAD
        k_ref[:, c0:c0 + MLA_HEAD_PAD] = (kmat[:, c0:c0 + MLA_HEAD_PAD] + kslab).astype(BF16)
    vt_ref[0] = _dot_nt(wvt_ref[...], kvn).astype(BF16)
    qt = _dot_nt(wqt_ref[...], qn) * ATT_Q_SCALE
    cos_t = cos_t_ref[0]
    sin_t = sin_t_ref[0]
    for h in range(MLA_HEADS):
        r0 = h * MLA_HEAD_PAD
        qt_ref[0, r0:r0 + MLA_NOPE, :] = qt[r0:r0 + MLA_NOPE, :].astype(BF16)
        t1 = qt[r0 + MLA_NOPE:r0 + MLA_NOPE + MLA_HALF, :]
        t2 = qt[r0 + MLA_NOPE + MLA_HALF:r0 + MLA_QK, :]
        qt_ref[0, r0 + MLA_NOPE:r0 + MLA_NOPE + MLA_HALF, :] = (t1 * cos_t - t2 * sin_t).astype(BF16)
        qt_ref[0, r0 + MLA_NOPE + MLA_HALF:r0 + MLA_QK, :] = (t2 * cos_t + t1 * sin_t).astype(BF16)
        qt_ref[0, r0 + MLA_QK:r0 + MLA_HEAD_PAD, :] = jnp.zeros((MLA_HEAD_PAD - MLA_QK, MLA_TM), BF16)


def _mla_proj(x, w_in_pad, gq, gkv, wqt, wk, wvt, cos_t, sin_t, cos_tab, sin_tab):
    tm = MLA_TM
    nl = SEQ // tm
    hp = MLA_HEADS * MLA_HEAD_PAD
    return pl.pallas_call(
        _mla_proj_kernel,
        grid=(BATCH, nl),
        in_specs=[
            pl.BlockSpec((tm, D_MODEL), lambda b, j: (b * nl + j, 0)),
            _const_spec((D_MODEL, MLA_LAT)),
            _const_spec((1, MLA_Q_RANK)),
            _const_spec((1, MLA_KV_RANK)),
            _const_spec((hp, MLA_Q_RANK)),
            _const_spec((MLA_KV_RANK, hp)),
            _const_spec((MLA_HEADS * MLA_V, MLA_KV_RANK)),
            pl.BlockSpec((1, MLA_HALF, tm), lambda b, j: (b, 0, j)),
            pl.BlockSpec((1, MLA_HALF, tm), lambda b, j: (b, 0, j)),
            pl.BlockSpec((tm, LANES), lambda b, j: (b * nl + j, 0)),
            pl.BlockSpec((tm, LANES), lambda b, j: (b * nl + j, 0)),
        ],
        out_specs=[
            pl.BlockSpec((1, hp, tm), lambda b, j: (b, 0, j)),
            pl.BlockSpec((tm, hp), lambda b, j: (b * nl + j, 0)),
            pl.BlockSpec((1, MLA_HEADS * MLA_V, tm), lambda b, j: (b, 0, j)),
        ],
        out_shape=[
            jax.ShapeDtypeStruct((BATCH, hp, SEQ), BF16),
            jax.ShapeDtypeStruct((TOKENS, hp), BF16),
            jax.ShapeDtypeStruct((BATCH, MLA_HEADS * MLA_V, SEQ), BF16),
        ],
        compiler_params=_params(("arbitrary", "arbitrary")),
        name="mla_proj",
    )(x, w_in_pad, gq, gkv, wqt, wk, wvt, cos_t, sin_t, cos_tab, sin_tab)


def _attn_kernel(qt_ref, k_ref, vt_ref, o_ref, q2_ref, s_ref, p_ref, acc_ref):
    t_ = ATT_T
    nh = ATT_HEADS_PER_STEP
    hp = MLA_HEAD_PAD

    q2_ref[0] = qt_ref[0]
    q2_ref[1] = qt_ref[0]
    crow = lax.broadcasted_iota(jnp.int32, (ATT_MASK_ROWS, t_), 0)
    qcol = lax.broadcasted_iota(jnp.int32, (ATT_MASK_ROWS, t_), 1)
    qchunk = (qcol // PERM_BLOCK * PERM_BLOCK + _perm_token(qcol)) // CHUNK
    mask_rows = jnp.where((crow < t_ // CHUNK) & (crow > qchunk), NEG_BIG, 0.0).astype(BF16)
    for hh in range(nh):
        r0 = hh * hp + MLA_QK
        for qq in range(ATT_NQ):
            q2_ref[1, r0:r0 + ATT_MASK_ROWS, qq * t_:(qq + 1) * t_] = mask_rows
    p_ref[...] = jnp.zeros(p_ref.shape, BF16)
    acc_ref[...] = jnp.zeros(acc_ref.shape, F32)

    def nxt(qi, j):
        last = j >= qi
        return jnp.where(last, qi + 1, qi), jnp.where(last, 0, j + 1)

    def qk(slot, qi, j):
        diag = (j == qi).astype(jnp.int32)
        k0 = pl.multiple_of(j * t_, t_)
        q0 = pl.multiple_of(qi * t_, t_)
        for hh in range(nh):
            kt = k_ref[pl.ds(k0, t_), hh * hp:(hh + 1) * hp]
            qv = q2_ref[diag, hh * hp:(hh + 1) * hp, pl.ds(q0, t_)]
            s_ref[slot, hh] = _dot(kt, qv)

    def sm(slot, j, m, l):
        fresh = j == 0
        m_out, l_out, a_out = [], [], []
        for hh in range(nh):
            s = s_ref[slot, hh]
            m_old = jnp.where(fresh, NEG_BIG, m[hh])
            l_old = jnp.where(fresh, 0.0, l[hh])
            m_new = jnp.maximum(m_old, jnp.max(s, axis=0, keepdims=True))
            alpha = jnp.exp2(m_old - m_new)
            p = jnp.exp2(s - m_new)
            l_out.append(alpha * l_old + jnp.sum(p, axis=0, keepdims=True))
            m_out.append(m_new)
            a_out.append(alpha)
            p_ref[slot, hh] = p.astype(BF16)
        return tuple(m_out), tuple(l_out), tuple(a_out)

    def pv(slot, qi, j, alpha, l):
        k0 = pl.multiple_of(j * t_, t_)
        q0 = pl.multiple_of(qi * t_, t_)
        for hh in range(nh):
            vt = vt_ref[0, hh * MLA_V:(hh + 1) * MLA_V, pl.ds(k0, t_)]
            upd = _dot(vt, p_ref[slot, hh])
            acc_old = jnp.where(j == 0, 0.0, acc_ref[hh])
            acc_new = acc_old * alpha[hh] + upd
            acc_ref[hh] = acc_new
            o_ref[0, hh * MLA_V:(hh + 1) * MLA_V, pl.ds(q0, t_)] = (acc_new * (1.0 / l[hh])).astype(BF16)

    def step(par, carry, do_qk=True, do_sm=True):
        qa, ja, qb, jb, qc, jc, m, l, al = carry
        pv(par, qc, jc, al, l)
        if do_sm:
            m2, l2, al2 = sm(1 - par, jb, m, l)
        else:
            m2, l2, al2 = m, l, al
        if do_qk:
            qk(par, qa, ja)
        qa2, ja2 = nxt(qa, ja)
        return (qa2, ja2, qa, ja, qb, jb, m2, l2, al2)

    zero = jnp.int32(0)
    qk(0, zero, zero)
    row = lambda v: tuple(jnp.full((1, t_), v, F32) for _ in range(nh))
    carry = (jnp.int32(1), zero, zero, zero, zero, zero, row(NEG_BIG), row(1.0), row(1.0))
    n_items = ATT_NQ * (ATT_NQ + 1) // 2
    assert n_items % 2 == 0
    carry = lax.fori_loop(0, (n_items - 2) // 2, lambda u, cr: step(0, step(1, cr)), carry)
    carry = step(1, carry)
    carry = step(0, carry, do_qk=False)
    step(1, carry, do_qk=False, do_sm=False)


def _attention(qt, k, vt):
    nh = ATT_HEADS_PER_STEP
    return pl.pallas_call(
        _attn_kernel,
        grid=(BATCH, MLA_HEADS // nh),
        in_specs=[
            pl.BlockSpec((1, nh * MLA_HEAD_PAD, SEQ), lambda b, h: (b, h, 0)),
            pl.BlockSpec((SEQ, nh * MLA_HEAD_PAD), lambda b, h: (b, h)),
            pl.BlockSpec((1, nh * MLA_V, SEQ), lambda b, h: (b, h, 0)),
        ],
        out_specs=pl.BlockSpec((1, nh * MLA_V, SEQ), lambda b, h: (b, h, 0)),
        out_shape=jax.ShapeDtypeStruct((BATCH, MLA_HEADS * MLA_V, SEQ), BF16),
        scratch_shapes=[
            pltpu.VMEM((2, nh * MLA_HEAD_PAD, SEQ), BF16),
            pltpu.VMEM((2, nh, ATT_T, ATT_T), F32),
            pltpu.VMEM((2, nh, ATT_T, ATT_T), BF16),
            pltpu.VMEM((nh, MLA_V, ATT_T), F32),
        ],
        compiler_params=_params(("arbitrary", "arbitrary")),
        name="mla_attention",
    )(qt, k, vt)


PROJ_TM = 512


def _proj_ln_kernel(x_ref, at_ref, w_ref, g_ref, b_ref, o_ref):
    y = lax.dot_general(at_ref[0], w_ref[...], (((0,), (0,)), ((), ())), preferred_element_type=F32)
    z = DN_ALPHA * x_ref[...] + y
    o_ref[...] = _layer_norm(z, g_ref[...], b_ref[...])


def _proj_ln(x, a_t, w, g, b):
    tm = PROJ_TM
    nl = SEQ // tm
    kdim = a_t.shape[1]
    return pl.pallas_call(
        _proj_ln_kernel,
        grid=(BATCH, nl),
        in_specs=[
            pl.BlockSpec((tm, D_MODEL), lambda bi, j: (bi * nl + j, 0)),
            pl.BlockSpec((1, kdim, tm), lambda bi, j: (bi, 0, j)),
            _const_spec((kdim, D_MODEL)),
            _const_spec((1, D_MODEL)),
            _const_spec((1, D_MODEL)),
        ],
        out_specs=pl.BlockSpec((tm, D_MODEL), lambda bi, j: (bi * nl + j, 0)),
        out_shape=jax.ShapeDtypeStruct((TOKENS, D_MODEL), F32),
        compiler_params=_params(("arbitrary", "arbitrary")),
        name="proj_ln",
    )(x, a_t, w, g, b)


SSM_TM = 512
SSD_Q = 128
SSD_TM = 512
LOG2E = math.log2(math.e)


def _ssm_in_kernel(x_ref, wz_ref, wxbc_ref, wdt_ref, cw_ref, cb_ref, dtb_ref,
                   z_ref, xs_ref, b_ref, bt_ref, c_ref, dt_ref, dtt_ref, buf_ref, halo_ref):
    tm = SSM_TM
    blk = CONV_BLK
    nblk = SSM_CONV_DIM // blk
    nz = SSM_D_INNER // blk

    @pl.when(pl.program_id(1) == 0)
    def _():
        halo_ref[...] = jnp.zeros(halo_ref.shape, F32)

    xb = x_ref[...].astype(BF16)
    dt = jax.nn.softplus(_dot(xb, wdt_ref[...]) + dtb_ref[...])
    dt_ref[...] = dt
    dtt_ref[0] = dt.T[0:SSM_HEADS, :]

    def project(k):
        _conv_store(buf_ref.at[k % 4], _dot(xb, wxbc_ref[:, k * blk:(k + 1) * blk]), SSM_CONV, tm)

    def gate_proj(k):
        z_ref[:, k * blk:(k + 1) * blk] = _dot(xb, wz_ref[:, k * blk:(k + 1) * blk]).astype(BF16)

    def conv(k):
        c0 = k * blk
        v = _silu(jnp.concatenate(_conv_block(buf_ref.at[k % 4], halo_ref, cw_ref, cb_ref, c0, SSM_CONV, tm), axis=0))
        if c0 < SSM_D_INNER:
            xs_ref[:, c0:c0 + blk] = v.astype(BF16)
        elif c0 < SSM_D_INNER + SSM_BC:
            n0 = c0 - SSM_D_INNER
            b_ref[:, n0:n0 + blk] = v.astype(BF16)
            bt_ref[0, n0:n0 + blk, :] = v.T.astype(BF16)
        else:
            n0 = c0 - SSM_D_INNER - SSM_BC
            c_ref[:, n0:n0 + blk] = v.astype(BF16)

    assert nblk == 2 * nz
    project(0)
    project(1)
    for k in range(0, nblk, 2):
        if k + 2 < nblk:
            project(k + 2)
            project(k + 3)
        gate_proj(k // 2)
        conv(k)
        conv(k + 1)


def _ssm_in(x, w_z, w_xbc, w_dt, conv_w, conv_b, dt_bias):
    tm = SSM_TM
    nl = SEQ // tm
    tok = lambda n: pl.BlockSpec((tm, n), lambda b, j: (b * nl + j, 0))
    return pl.pallas_call(
        _ssm_in_kernel,
        grid=(BATCH, nl),
        in_specs=[
            tok(D_MODEL),
            _const_spec((D_MODEL, SSM_D_INNER)),
            _const_spec((D_MODEL, SSM_CONV_DIM)),
            _const_spec((D_MODEL, LANES)),
            _const_spec((SSM_CONV, SSM_CONV_DIM)),
            _const_spec((1, SSM_CONV_DIM)),
            _const_spec((1, LANES)),
        ],
        out_specs=[
            tok(SSM_D_INNER),
            tok(SSM_D_INNER),
            tok(SSM_BC),
            pl.BlockSpec((1, SSM_BC, tm), lambda b, j: (b, 0, j)),
            tok(SSM_BC),
            tok(LANES),
            pl.BlockSpec((1, SSM_HEADS, tm), lambda b, j: (b, 0, j)),
        ],
        out_shape=[
            jax.ShapeDtypeStruct((TOKENS, SSM_D_INNER), BF16),
            jax.ShapeDtypeStruct((TOKENS, SSM_D_INNER), BF16),
            jax.ShapeDtypeStruct((TOKENS, SSM_BC), BF16),
            jax.ShapeDtypeStruct((BATCH, SSM_BC, SEQ), BF16),
            jax.ShapeDtypeStruct((TOKENS, SSM_BC), BF16),
            jax.ShapeDtypeStruct((TOKENS, LANES), F32),
            jax.ShapeDtypeStruct((BATCH, SSM_HEADS, SEQ), F32),
        ],
        scratch_shapes=[
            pltpu.VMEM((4, _conv_rows(SSM_CONV, tm), CONV_BLK), F32),
            pltpu.VMEM(((SSM_CONV - 1) * SUBLANES, SSM_CONV_DIM), F32),
        ],
        compiler_params=_params(("arbitrary", "arbitrary")),
        name="ssm_in",
    )(x, w_z, w_xbc, w_dt, conv_w, conv_b, dt_bias)


def _ssd_kernel(x_ref, z_ref, xs_ref, b_ref, bt_ref, c_ref, dt_ref, dtt_ref,
                alog_row_ref, alog_col_ref, dskip_ref, ng_ref, rexp_ref, wout_ref, g_ref, bb_ref,
                o_ref, state_ref, y_ref, xbd_ref, yb_ref):
    q = SSD_Q
    hi = lax.Precision.HIGHEST
    hw = SSM_HPG * SSM_HEAD_DIM

    @pl.when(pl.program_id(1) == 0)
    def _():
        state_ref[...] = jnp.zeros(state_ref.shape, F32)
        xbd_ref[...] = jnp.zeros(xbd_ref.shape, BF16)

    row = lax.broadcasted_iota(jnp.int32, (q, q), 0)
    col = lax.broadcasted_iota(jnp.int32, (q, q), 1)
    lower = _perm_token(col) <= _perm_token(row)
    tri = lower.astype(F32)
    a_row = -jnp.exp(alog_row_ref[...]) * LOG2E
    a_col = -jnp.exp(alog_col_ref[...]) * LOG2E
    lane_head = lax.broadcasted_iota(jnp.int32, (1, hw), 1) // SSM_HEAD_DIM

    def scan(c):
        rows = slice(c * q, (c + 1) * q)
        dt = dt_ref[rows, :]
        dt_t = dtt_ref[0, :, rows]
        acum = jnp.dot(tri, dt * a_row, precision=hi, preferred_element_type=F32)
        acum_t = lax.dot_general(dt_t * a_col, tri, (((1,), (1,)), ((), ())), precision=hi,
                                 preferred_element_type=F32)
        e_col = jnp.exp2(acum).astype(BF16)
        w_col = (jnp.exp2(acum[q - 1:q, :] - acum) * dt).astype(BF16)
        rowterm = acum_t - jnp.log2(dt_t)
        cd = jnp.exp2(acum_t[:, q - 1:q])
        for grp in range(SSM_GROUPS):
            n0 = grp * SSM_STATE
            c0 = grp * hw
            cg = c_ref[rows, n0:n0 + SSM_STATE]
            cb = _dot_nt(cg, b_ref[rows, n0:n0 + SSM_STATE])
            rg = rexp_ref[:, c0:c0 + hw]
            eg = _dot(e_col, rg)
            wg = _dot(w_col, rg)
            ms = []
            cdg = jnp.zeros((1, hw), F32)
            for r in range(SSM_HPG):
                h = grp * SSM_HPG + r
                p0 = c0 + r * SSM_HEAD_DIM
                seg = acum[:, h:h + 1] - rowterm[h:h + 1, :]
                ms.append((cb * jnp.exp2(jnp.where(lower, seg, NEG_BIG))).astype(BF16))
                xbd_ref[c, grp, r * q:(r + 1) * q, r * SSM_HEAD_DIM:(r + 1) * SSM_HEAD_DIM] = xs_ref[rows, p0:p0 + SSM_HEAD_DIM]
                cdg = jnp.where(lane_head == r, cd[h:h + 1, :], cdg)
            s_old = state_ref[grp]
            xgf = xs_ref[rows, c0:c0 + hw].astype(F32)
            yg = (_dot(jnp.concatenate(ms, axis=1), xbd_ref[c, grp])
                  + _dot(cg, s_old.astype(BF16)) * eg + dskip_ref[:, c0:c0 + hw] * xgf)
            y_ref[rows, c0:c0 + hw] = yg
            state_ref[grp] = s_old * cdg + _dot(bt_ref[0, n0:n0 + SSM_STATE, rows], (xgf * wg).astype(BF16))

    def finish(c):
        rows = slice(c * q, (c + 1) * q)
        y = y_ref[rows, :] * _silu(z_ref[rows, :].astype(F32))
        for grp in range(SSM_GROUPS):
            c0 = grp * hw
            yb_ref[rows, c0:c0 + hw] = _rms(y[:, c0:c0 + hw], ng_ref[:, c0:c0 + hw]).astype(BF16)
        zres = DN_ALPHA * x_ref[rows, :] + _dot(yb_ref[rows, :], wout_ref[...])
        o_ref[rows, :] = _layer_norm(zres, g_ref[...], bb_ref[...])

    nc = SSD_TM // q
    for c in range(nc + 1):
        if c < nc:
            scan(c)
        if c >= 1:
            finish(c - 1)


def _ssd(x, z, xs, bm, bt, cm, dt, dtt, alog_row, alog_col, dskip, ng, rexp, w_out, g, b):
    tm = SSD_TM
    nl = SEQ // tm
    tok = lambda n: pl.BlockSpec((tm, n), lambda bi, j: (bi * nl + j, 0))
    return pl.pallas_call(
        _ssd_kernel,
        grid=(BATCH, nl),
        in_specs=[
            tok(D_MODEL),
            tok(SSM_D_INNER),
            tok(SSM_D_INNER),
            tok(SSM_BC),
            pl.BlockSpec((1, SSM_BC, tm), lambda bi, j: (bi, 0, j)),
            tok(SSM_BC),
            tok(LANES),
            pl.BlockSpec((1, SSM_HEADS, tm), lambda bi, j: (bi, 0, j)),
            _const_spec((1, LANES)),
            _const_spec((SSM_HEADS, 1)),
            _const_spec((1, SSM_D_INNER)),
            _const_spec((1, SSM_D_INNER)),
            _const_spec((LANES, SSM_D_INNER)),
            _const_spec((SSM_D_INNER, D_MODEL)),
            _const_spec((1, D_MODEL)),
            _const_spec((1, D_MODEL)),
        ],
        out_specs=tok(D_MODEL),
        out_shape=jax.ShapeDtypeStruct((TOKENS, D_MODEL), F32),
        scratch_shapes=[
            pltpu.VMEM((SSM_GROUPS, SSM_STATE, SSM_HPG * SSM_HEAD_DIM), F32),
            pltpu.VMEM((tm, SSM_D_INNER), F32),
            pltpu.VMEM((tm // SSD_Q, SSM_GROUPS, SSM_HPG * SSD_Q, SSM_HPG * SSM_HEAD_DIM), BF16),
            pltpu.VMEM((tm, SSM_D_INNER), BF16),
        ],
        compiler_params=_params(("arbitrary", "arbitrary")),
        name="ssd_scan",
    )(x, z, xs, bm, bt, cm, dt, dtt, alog_row, alog_col, dskip, ng, rexp, w_out, g, b)


def _row(v):
    return v.reshape(1, -1).astype(F32)


def _mamba_layer(x, w_in, conv_w, conv_b, dt_bias, a_log, d_skip, norm_g, w_out, g, b):
    w_z = w_in[:, :SSM_D_INNER].astype(BF16)
    w_xbc = w_in[:, SSM_D_INNER:SSM_D_INNER + SSM_CONV_DIM].astype(BF16)
    w_dt = jnp.pad(w_in[:, SSM_D_INNER + SSM_CONV_DIM:], ((0, 0), (0, LANES - SSM_HEADS))).astype(BF16)
    dtb = jnp.pad(dt_bias, (0, LANES - SSM_HEADS)).reshape(1, LANES)
    z, xs, bm, bt, cm, dt, dtt = _ssm_in(x, w_z, w_xbc, w_dt, conv_w, _row(conv_b), dtb)
    alog_row = jnp.pad(a_log, (0, LANES - SSM_HEADS)).reshape(1, LANES)
    alog_col = a_log.reshape(SSM_HEADS, 1)
    dskip = jnp.repeat(d_skip, SSM_HEAD_DIM).reshape(1, SSM_D_INNER)
    rexp = jnp.repeat(jnp.eye(LANES, SSM_HEADS, dtype=F32), SSM_HEAD_DIM, axis=1).astype(BF16)
    return _ssd(x, z, xs, bm, bt, cm, dt, dtt, alog_row, alog_col, dskip, _row(norm_g), rexp,
                w_out.astype(BF16), _row(g), _row(b))


def _sg_layer(x, w_in, b_in, ln_g, ln_b, w_s, b_s, w_out, g, b):
    tok = _perm_token(jnp.arange(SG_BLOCK))
    b_s_wide = jnp.repeat(b_s.T[tok], SG_GROUP_DIM, axis=1)
    w_s = w_s[:, tok][:, :, tok]
    return _spatial_gating(x, w_in.astype(BF16), _row(b_in), _row(ln_g), _row(ln_b), w_s, b_s_wide,
                           w_out.astype(BF16), _row(g), _row(b))


def _mla_layer(x, positions, w_in, q_norm_g, w_q_b, kv_norm_g, w_kv_b, w_out, g, b):
    h = MLA_HEADS
    s0 = MLA_Q_RANK + MLA_KV_RANK
    t1 = w_in[:, s0:s0 + MLA_HALF]
    t2 = w_in[:, s0 + MLA_HALF:s0 + MLA_ROPE]
    zl = jnp.zeros((D_MODEL, MLA_NOPE), F32)
    zh = jnp.zeros((D_MODEL, LANES - MLA_QK), F32)
    w_in_pad = jnp.concatenate([w_in[:, :s0], zl, t1, t2, zh, zl, -t2, t1, zh], axis=1).astype(BF16)

    wq = w_q_b.reshape(MLA_Q_RANK, h, MLA_QK)
    wq = jnp.pad(wq, ((0, 0), (0, 0), (0, MLA_HEAD_PAD - MLA_QK)))
    wqt = wq.reshape(MLA_Q_RANK, h * MLA_HEAD_PAD).T.astype(BF16)
    wkv = w_kv_b.reshape(MLA_KV_RANK, h, MLA_NOPE + MLA_V)
    wk = jnp.pad(wkv[:, :, :MLA_NOPE], ((0, 0), (0, 0), (0, MLA_HEAD_PAD - MLA_NOPE)))
    wk = wk.reshape(MLA_KV_RANK, h * MLA_HEAD_PAD).astype(BF16)
    wvt = wkv[:, :, MLA_NOPE:].reshape(MLA_KV_RANK, h * MLA_V).T.astype(BF16)

    inv = (ROPE_THETA ** (-(jnp.arange(MLA_HALF, dtype=F32) * 2.0 / MLA_ROPE))).reshape(MLA_HALF, 1)
    cos_t, sin_t, cos_tab, sin_tab = _rope_tables(positions, inv)
    qt, k, vt = _mla_proj(x, w_in_pad, _row(q_norm_g), _row(kv_norm_g), wqt, wk, wvt,
                          cos_t, sin_t, cos_tab, sin_tab)
    o = _attention(qt, k, vt)
    return _proj_ln(x, o, w_out.astype(BF16), _row(g), _row(b))


def kernel(x, positions, ssm_w_in, ssm_conv_w, ssm_conv_b, ssm_dt_bias, ssm_a_log, ssm_d, ssm_norm_g, ssm_w_out, sg_w_in, sg_b_in, sg_ln_g, sg_ln_b, sg_w_s, sg_b_s, sg_w_out, mla_w_in, mla_q_norm_g, mla_w_q_b, mla_kv_norm_g, mla_w_kv_b, mla_w_out, ffn_w_in, ffn_conv_w, ffn_conv_b, ffn_w_out, ln_g, ln_b):
    h = _permute_tokens(x).reshape(TOKENS, D_MODEL)
    positions = _permute_tokens(positions)
    for i in range(DEPTH):
        m, j = i % N_MIXERS, i // N_MIXERS
        g0, b0 = ln_g[i, 0], ln_b[i, 0]
        if m == 0:
            h = _mamba_layer(h, ssm_w_in[j], ssm_conv_w[j], ssm_conv_b[j], ssm_dt_bias[j], ssm_a_log[j],
                             ssm_d[j], ssm_norm_g[j], ssm_w_out[j], g0, b0)
        elif m == 1:
            h = _sg_layer(h, sg_w_in[j], sg_b_in[j], sg_ln_g[j], sg_ln_b[j], sg_w_s[j], sg_b_s[j],
                          sg_w_out[j], g0, b0)
        else:
            h = _mla_layer(h, positions, mla_w_in[j], mla_q_norm_g[j], mla_w_q_b[j], mla_kv_norm_g[j],
                           mla_w_kv_b[j], mla_w_out[j], g0, b0)
        h = _ffn(h, ffn_w_in[i].astype(BF16), ffn_conv_w[i], _row(ffn_conv_b[i]),
                 ffn_w_out[i].astype(BF16), _row(ln_g[i, 1]), _row(ln_b[i, 1]))
    return _unpermute_tokens(h.reshape(BATCH, SEQ, D_MODEL))
```

```python
import functools
import math

import jax
import jax.numpy as jnp
from jax import lax
from jax.experimental import pallas as pl
from jax.experimental.pallas import tpu as pltpu

D_MODEL = 1024
BATCH = 8
SEQ = 4096
DEPTH = 4
TOKENS = BATCH * SEQ

CHUNK = 64
N_MIXERS = 3

DN_ALPHA = (2.0 * DEPTH) ** 0.25
LN_EPS = 1e-5
RMS_EPS = 1e-6

SSM_D_INNER = 2048
SSM_HEAD_DIM = 64
SSM_HEADS = 32
SSM_GROUPS = 8
SSM_HPG = 4
SSM_STATE = 128
SSM_CONV = 4
SSM_CONV_DIM = 4096
SSM_BC = SSM_GROUPS * SSM_STATE

SG_BLOCK = 128
SG_WIDTH = 2048
SG_GROUPS = 8
SG_GROUP_DIM = 256

MLA_HEADS = 16
MLA_Q_RANK = 384
MLA_KV_RANK = 256
MLA_NOPE = 64
MLA_ROPE = 32
MLA_HALF = 16
MLA_V = 64
MLA_QK = MLA_NOPE + MLA_ROPE
ROPE_THETA = 10000.0

FFN_HIDDEN = 2816
FFN_CONV = 3

LANES = 128
SUBLANES = 8
PERM_BLOCK = 128
PERM_GROUPS = PERM_BLOCK // SUBLANES
VMEM_LIMIT = 56 * 1024 * 1024

BF16 = jnp.bfloat16
F32 = jnp.float32
NEG_BIG = -1e30


def _dot(a, b):
    return jnp.dot(a, b, preferred_element_type=F32)


def _dot_nt(a, b):
    return lax.dot_general(a, b, (((1,), (1,)), ((), ())), preferred_element_type=F32)


def _layer_norm(z, g, b):
    mu = jnp.mean(z, axis=-1, keepdims=True)
    zc = z - mu
    var = jnp.mean(zc * zc, axis=-1, keepdims=True)
    return zc * lax.rsqrt(var + LN_EPS) * g + b


def _silu(x):
    return x * (1.0 / (1.0 + jnp.exp(-x)))


def _const_spec(shape):
    nd = len(shape)
    return pl.BlockSpec(shape, lambda *_: (0,) * nd, pipeline_mode=pl.Buffered(1))


def _params(sem):
    return pltpu.CompilerParams(dimension_semantics=sem, vmem_limit_bytes=VMEM_LIMIT)


FFN_TM = 512
CONV_BLK = 2 * LANES
FFN_NBLK = FFN_HIDDEN // CONV_BLK
FFN_OUT_GROUP = 3
FFN_NGROUP = -(-FFN_NBLK // FFN_OUT_GROUP)


def _permute_tokens(a):
    shp = a.shape
    a = a.reshape(shp[0], shp[1] // PERM_BLOCK, SUBLANES, PERM_GROUPS, *shp[2:])
    return jnp.swapaxes(a, 2, 3).reshape(shp)


def _unpermute_tokens(a):
    shp = a.shape
    a = a.reshape(shp[0], shp[1] // PERM_BLOCK, PERM_GROUPS, SUBLANES, *shp[2:])
    return jnp.swapaxes(a, 2, 3).reshape(shp)


def _perm_token(idx):
    return (idx % SUBLANES) * PERM_GROUPS + (idx % PERM_BLOCK) // SUBLANES


def _conv_rows(taps, tm):
    return (tm // PERM_BLOCK) * (PERM_BLOCK + (taps - 1) * SUBLANES)


def _conv_store(buf, res, taps, tm):
    pad = (taps - 1) * SUBLANES
    stride = PERM_BLOCK + pad
    for blk in range(tm // PERM_BLOCK):
        buf[blk * stride + pad:(blk + 1) * stride, :] = res[blk * PERM_BLOCK:(blk + 1) * PERM_BLOCK, :]


def _conv_block(buf, halo_ref, cw_ref, cb_ref, c0, taps, tm):
    pad = (taps - 1) * SUBLANES
    stride = PERM_BLOCK + pad
    nb = tm // PERM_BLOCK
    top = lax.broadcasted_iota(jnp.int32, (SUBLANES, CONV_BLK), 0) == 0
    outs = []
    for blk in range(nb):
        base = blk * stride
        for s in range(1, taps):
            g0 = base + stride - s * SUBLANES
            cur = buf[g0:g0 + SUBLANES, :]
            if blk == 0:
                prev = halo_ref[(s - 1) * SUBLANES:s * SUBLANES, c0:c0 + CONV_BLK]
            else:
                prev = buf[base - s * SUBLANES:base - (s - 1) * SUBLANES, :]
            v0 = base + pad - s * SUBLANES
            buf[v0:v0 + SUBLANES, :] = jnp.where(top, pltpu.roll(prev, 1, 0), pltpu.roll(cur, 1, 0))
        acc = cb_ref[:, c0:c0 + CONV_BLK]
        for t in range(taps):
            r0 = base + t * SUBLANES
            acc = acc + cw_ref[t:t + 1, c0:c0 + CONV_BLK] * buf[r0:r0 + PERM_BLOCK, :]
        outs.append(acc)
    for s in range(1, taps):
        g0 = nb * stride - s * SUBLANES
        halo_ref[(s - 1) * SUBLANES:s * SUBLANES, c0:c0 + CONV_BLK] = buf[g0:g0 + SUBLANES, :]
    return outs


def _ffn_kernel(x_ref, win_ref, cw_ref, cb_ref, wout_ref, g_ref, b_ref, o_ref,
                buf_ref, halo_ref, act_ref, acc_ref, xb_ref):
    tm = FFN_TM

    @pl.when(pl.program_id(0) % (SEQ // tm) == 0)
    def _():
        halo_ref[...] = jnp.zeros(halo_ref.shape, F32)

    xb_ref[...] = x_ref[...].astype(BF16)

    def project(k):
        for half in range(2):
            off = half * FFN_HIDDEN + k * CONV_BLK
            _conv_store(buf_ref.at[k % 2, half], _dot(xb_ref[...], win_ref[:, off:off + CONV_BLK]), FFN_CONV, tm)

    def gate(k):
        conv = [_conv_block(buf_ref.at[k % 2, half], halo_ref, cw_ref, cb_ref,
                            half * FFN_HIDDEN + k * CONV_BLK, FFN_CONV, tm) for half in range(2)]
        j = k % FFN_OUT_GROUP
        for blk in range(tm // PERM_BLOCK):
            act_ref[k // FFN_OUT_GROUP, blk * PERM_BLOCK:(blk + 1) * PERM_BLOCK, j * CONV_BLK:(j + 1) * CONV_BLK] = (
                _silu(conv[0][blk]) * conv[1][blk]).astype(BF16)

    def contract(grp):
        k0 = grp * FFN_OUT_GROUP
        n = min(FFN_OUT_GROUP, FFN_NBLK - k0) * CONV_BLK
        part = _dot(act_ref[grp, :, 0:n], wout_ref[k0 * CONV_BLK:k0 * CONV_BLK + n, :])
        if grp == 0:
            acc_ref[...] = part
        else:
            acc_ref[...] += part

    project(0)
    for k in range(FFN_NBLK):
        if k + 1 < FFN_NBLK:
            project(k + 1)
        gate(k)
        if (k + 1) % FFN_OUT_GROUP == 0 or k + 1 == FFN_NBLK:
            contract(k // FFN_OUT_GROUP)
    z = DN_ALPHA * x_ref[...] + acc_ref[...]
    o_ref[...] = _layer_norm(z, g_ref[...], b_ref[...])


def _ffn(x, w_in, conv_w, conv_b, w_out, g, b):
    tm = FFN_TM
    return pl.pallas_call(
        _ffn_kernel,
        grid=(TOKENS // tm,),
        in_specs=[
            pl.BlockSpec((tm, D_MODEL), lambda i: (i, 0)),
            _const_spec((D_MODEL, 2 * FFN_HIDDEN)),
            _const_spec((FFN_CONV, 2 * FFN_HIDDEN)),
            _const_spec((1, 2 * FFN_HIDDEN)),
            _const_spec((FFN_HIDDEN, D_MODEL)),
            _const_spec((1, D_MODEL)),
            _const_spec((1, D_MODEL)),
        ],
        out_specs=pl.BlockSpec((tm, D_MODEL), lambda i: (i, 0)),
        out_shape=jax.ShapeDtypeStruct((TOKENS, D_MODEL), F32),
        scratch_shapes=[
            pltpu.VMEM((2, 2, _conv_rows(FFN_CONV, tm), CONV_BLK), F32),
            pltpu.VMEM(((FFN_CONV - 1) * SUBLANES, 2 * FFN_HIDDEN), F32),
            pltpu.VMEM((FFN_NGROUP, tm, FFN_OUT_GROUP * CONV_BLK), BF16),
            pltpu.VMEM((tm, D_MODEL), F32),
            pltpu.VMEM((tm, D_MODEL), BF16),
        ],
        compiler_params=_params(("arbitrary",)),
        name="conv_ffn",
    )(x, w_in, conv_w, conv_b, w_out, g, b)


SG_TM = 512
SG_PROJ_BLK = 4 * LANES


def _sg_kernel(x_ref, win_ref, bin_ref, lng_ref, lnb_ref, ws_ref, bs_ref, wout_ref, g_ref, b_ref,
               o_ref, v32_ref, v_ref, gated_ref, h_ref, xb_ref, acc_ref):
    tm = SG_TM
    xb_ref[...] = x_ref[...].astype(BF16)
    nhalf = SG_WIDTH // SG_PROJ_BLK
    gpb = SG_PROJ_BLK // SG_GROUP_DIM

    def project(k):
        c0 = k * SG_PROJ_BLK
        h_ref[k % 2] = _dot(xb_ref[...], win_ref[:, c0:c0 + SG_PROJ_BLK]) + bin_ref[:, c0:c0 + SG_PROJ_BLK]

    project(nhalf)
    for k in range(nhalf, 2 * nhalf):
        project(k + 1 if k + 1 < 2 * nhalf else 0)
        c0 = (k - nhalf) * SG_PROJ_BLK
        v32_ref[:, c0:c0 + SG_PROJ_BLK] = jax.nn.gelu(h_ref[k % 2])
    v_ref[...] = _layer_norm(v32_ref[...], lng_ref[...], lnb_ref[...]).astype(BF16)

    row = lax.broadcasted_iota(jnp.int32, (SG_BLOCK, SG_BLOCK), 0)
    col = lax.broadcasted_iota(jnp.int32, (SG_BLOCK, SG_BLOCK), 1)
    causal = _perm_token(col) <= _perm_token(row)
    for k in range(nhalf):
        if k + 1 < nhalf:
            project(k + 1)
        u = jax.nn.gelu(h_ref[k % 2])
        for gi in range(gpb):
            grp = k * gpb + gi
            c0 = grp * SG_GROUP_DIM
            ws = jnp.where(causal, ws_ref[grp], 0.0).astype(BF16)
            bias = bs_ref[:, c0:c0 + SG_GROUP_DIM]
            for blk in range(tm // SG_BLOCK):
                r0 = blk * SG_BLOCK
                mixed = _dot(ws, v_ref[r0:r0 + SG_BLOCK, c0:c0 + SG_GROUP_DIM]) + bias
                gated_ref[k, r0:r0 + SG_BLOCK, gi * SG_GROUP_DIM:(gi + 1) * SG_GROUP_DIM] = (
                    u[r0:r0 + SG_BLOCK, gi * SG_GROUP_DIM:(gi + 1) * SG_GROUP_DIM] * mixed).astype(BF16)
        part = _dot(gated_ref[k], wout_ref[k * SG_PROJ_BLK:(k + 1) * SG_PROJ_BLK, :])
        if k == 0:
            acc_ref[...] = part
        else:
            acc_ref[...] += part
    z = DN_ALPHA * x_ref[...] + acc_ref[...]
    o_ref[...] = _layer_norm(z, g_ref[...], b_ref[...])


def _spatial_gating(x, w_in, b_in, ln_g, ln_b, w_s, b_s_wide, w_out, g, b):
    tm = SG_TM
    return pl.pallas_call(
        _sg_kernel,
        grid=(TOKENS // tm,),
        in_specs=[
            pl.BlockSpec((tm, D_MODEL), lambda i: (i, 0)),
            _const_spec((D_MODEL, 2 * SG_WIDTH)),
            _const_spec((1, 2 * SG_WIDTH)),
            _const_spec((1, SG_WIDTH)),
            _const_spec((1, SG_WIDTH)),
            _const_spec((SG_GROUPS, SG_BLOCK, SG_BLOCK)),
            _const_spec((SG_BLOCK, SG_WIDTH)),
            _const_spec((SG_WIDTH, D_MODEL)),
            _const_spec((1, D_MODEL)),
            _const_spec((1, D_MODEL)),
        ],
        out_specs=pl.BlockSpec((tm, D_MODEL), lambda i: (i, 0)),
        out_shape=jax.ShapeDtypeStruct((TOKENS, D_MODEL), F32),
        scratch_shapes=[
            pltpu.VMEM((tm, SG_WIDTH), F32),
            pltpu.VMEM((tm, SG_WIDTH), BF16),
            pltpu.VMEM((SG_WIDTH // SG_PROJ_BLK, tm, SG_PROJ_BLK), BF16),
            pltpu.VMEM((2, tm, SG_PROJ_BLK), F32),
            pltpu.VMEM((tm, D_MODEL), BF16),
            pltpu.VMEM((tm, D_MODEL), F32),
        ],
        compiler_params=_params(("arbitrary",)),
        name="spatial_gating",
    )(x, w_in, b_in, ln_g, ln_b, w_s, b_s_wide, w_out, g, b)


ROPE_TL = 512
MLA_TM = 512
MLA_LAT = MLA_Q_RANK + MLA_KV_RANK + 2 * LANES
MLA_HEAD_PAD = LANES
ATT_T = 256
ATT_NQ = SEQ // ATT_T
ATT_HEADS_PER_STEP = 4
ATT_MASK_ROWS = 16
ATT_Q_SCALE = (MLA_QK ** -0.5) * math.log2(math.e)
ATT_V_PAD = 80


def _rope_kernel(pos_ref, inv_ref, cos_t_ref, sin_t_ref, cos_tab_ref, sin_tab_ref):
    ang = inv_ref[...] * pos_ref[0].astype(F32)
    c = jnp.cos(ang)
    s = jnp.sin(ang)
    cos_t_ref[0] = c
    sin_t_ref[0] = s
    zlo = jnp.zeros((MLA_NOPE, ROPE_TL), F32)
    zhi = jnp.zeros((LANES - MLA_QK, ROPE_TL), F32)
    cos_tab_ref[...] = jnp.concatenate([zlo, c, c, zhi], axis=0).T
    sin_tab_ref[...] = jnp.concatenate([zlo, s, s, zhi], axis=0).T


def _rope_tables(positions, inv):
    nl = SEQ // ROPE_TL
    pos3 = positions.reshape(BATCH, 1, SEQ)
    return pl.pallas_call(
        _rope_kernel,
        grid=(BATCH, nl),
        in_specs=[
            pl.BlockSpec((1, 1, ROPE_TL), lambda b, j: (b, 0, j)),
            pl.BlockSpec((MLA_HALF, 1), lambda b, j: (0, 0)),
        ],
        out_specs=[
            pl.BlockSpec((1, MLA_HALF, ROPE_TL), lambda b, j: (b, 0, j)),
            pl.BlockSpec((1, MLA_HALF, ROPE_TL), lambda b, j: (b, 0, j)),
            pl.BlockSpec((ROPE_TL, LANES), lambda b, j: (b * nl + j, 0)),
            pl.BlockSpec((ROPE_TL, LANES), lambda b, j: (b * nl + j, 0)),
        ],
        out_shape=[
            jax.ShapeDtypeStruct((BATCH, MLA_HALF, SEQ), F32),
            jax.ShapeDtypeStruct((BATCH, MLA_HALF, SEQ), F32),
            jax.ShapeDtypeStruct((TOKENS, LANES), F32),
            jax.ShapeDtypeStruct((TOKENS, LANES), F32),
        ],
        compiler_params=_params(("arbitrary", "arbitrary")),
        name="rope_tables",
    )(pos3, inv)


def _rms(x, g):
    return x * lax.rsqrt(jnp.mean(x * x, axis=-1, keepdims=True) + RMS_EPS) * g


def _mla_proj_kernel(x_ref, win_ref, gq_ref, gkv_ref, wqt_ref, wk_ref, wvt_ref,
                     cos_t_ref, sin_t_ref, cos_tab_ref, sin_tab_ref,
                     qt_ref, k_ref, vt_ref):
    xb = x_ref[...].astype(BF16)
    lat = _dot(xb, win_ref[...])
    qn = _rms(lat[:, 0:MLA_Q_RANK], gq_ref[...]).astype(BF16)
    kvn = _rms(lat[:, MLA_Q_RANK:MLA_Q_RANK + MLA_KV_RANK], gkv_ref[...]).astype(BF16)
    s0 = MLA_Q_RANK + MLA_KV_RANK
    kslab = lat[:, s0:s0 + LANES] * cos_tab_ref[...] + lat[:, s0 + LANES:s0 + 2 * LANES] * sin_tab_ref[...]
    lane = lax.broadcasted_iota(jnp.int32, (MLA_TM, LANES), 1) - MLA_QK
    rowi = lax.broadcasted_iota(jnp.int32, (MLA_TM, LANES), 0)
    chunk = ((rowi % ATT_T) // PERM_BLOCK * PERM_BLOCK + _perm_token(rowi)) // CHUNK
    kslab = kslab + jnp.where(lane == chunk, 1.0, 0.0)
    kmat = _dot(kvn, wk_ref[...])
    for h in range(MLA_HEADS):
        c0 = h * MLA_HEAD_PAD
        k_ref[:, c0:c0 + MLA_HEAD_PAD] = (kmat[:, c0:c0 + MLA_HEAD_PAD] + kslab).astype(BF16)
    vt = _dot_nt(wvt_ref[...], kvn)
    ones_row = (lax.broadcasted_iota(jnp.int32, (ATT_V_PAD - MLA_V, MLA_TM), 0) == 0).astype(BF16)
    for h in range(MLA_HEADS):
        vt_ref[0, h * ATT_V_PAD:h * ATT_V_PAD + MLA_V, :] = vt[h * MLA_V:(h + 1) * MLA_V, :].astype(BF16)
        vt_ref[0, h * ATT_V_PAD + MLA_V:(h + 1) * ATT_V_PAD, :] = ones_row
    qt = _dot_nt(wqt_ref[...], qn) * ATT_Q_SCALE
    cos_t = cos_t_ref[0]
    sin_t = sin_t_ref[0]
    for h in range(MLA_HEADS):
        r0 = h * MLA_HEAD_PAD
        qt_ref[0, r0:r0 + MLA_NOPE, :] = qt[r0:r0 + MLA_NOPE, :].astype(BF16)
        t1 = qt[r0 + MLA_NOPE:r0 + MLA_NOPE + MLA_HALF, :]
        t2 = qt[r0 + MLA_NOPE + MLA_HALF:r0 + MLA_QK, :]
        qt_ref[0, r0 + MLA_NOPE:r0 + MLA_NOPE + MLA_HALF, :] = (t1 * cos_t - t2 * sin_t).astype(BF16)
        qt_ref[0, r0 + MLA_NOPE + MLA_HALF:r0 + MLA_QK, :] = (t2 * cos_t + t1 * sin_t).astype(BF16)
        qt_ref[0, r0 + MLA_QK:r0 + MLA_HEAD_PAD, :] = jnp.zeros((MLA_HEAD_PAD - MLA_QK, MLA_TM), BF16)


def _mla_proj(x, w_in_pad, gq, gkv, wqt, wk, wvt, cos_t, sin_t, cos_tab, sin_tab):
    tm = MLA_TM
    nl = SEQ // tm
    hp = MLA_HEADS * MLA_HEAD_PAD
    return pl.pallas_call(
        _mla_proj_kernel,
        grid=(BATCH, nl),
        in_specs=[
            pl.BlockSpec((tm, D_MODEL), lambda b, j: (b * nl + j, 0)),
            _const_spec((D_MODEL, MLA_LAT)),
            _const_spec((1, MLA_Q_RANK)),
            _const_spec((1, MLA_KV_RANK)),
            _const_spec((hp, MLA_Q_RANK)),
            _const_spec((MLA_KV_RANK, hp)),
            _const_spec((MLA_HEADS * MLA_V, MLA_KV_RANK)),
            pl.BlockSpec((1, MLA_HALF, tm), lambda b, j: (b, 0, j)),
            pl.BlockSpec((1, MLA_HALF, tm), lambda b, j: (b, 0, j)),
            pl.BlockSpec((tm, LANES), lambda b, j: (b * nl + j, 0)),
            pl.BlockSpec((tm, LANES), lambda b, j: (b * nl + j, 0)),
        ],
        out_specs=[
            pl.BlockSpec((1, hp, tm), lambda b, j: (b, 0, j)),
            pl.BlockSpec((tm, hp), lambda b, j: (b * nl + j, 0)),
            pl.BlockSpec((1, MLA_HEADS * ATT_V_PAD, tm), lambda b, j: (b, 0, j)),
        ],
        out_shape=[
            jax.ShapeDtypeStruct((BATCH, hp, SEQ), BF16),
            jax.ShapeDtypeStruct((TOKENS, hp), BF16),
            jax.ShapeDtypeStruct((BATCH, MLA_HEADS * ATT_V_PAD, SEQ), BF16),
        ],
        compiler_params=_params(("arbitrary", "arbitrary")),
        name="mla_proj",
    )(x, w_in_pad, gq, gkv, wqt, wk, wvt, cos_t, sin_t, cos_tab, sin_tab)


def _attn_kernel(qt_ref, k_ref, vt_ref, o_ref, q2_ref, s_ref, p_ref, acc_ref, mt_ref):
    t_ = ATT_T
    nh = ATT_HEADS_PER_STEP
    hp = MLA_HEAD_PAD

    q2_ref[0] = qt_ref[0]
    q2_ref[1] = qt_ref[0]
    crow = lax.broadcasted_iota(jnp.int32, (ATT_MASK_ROWS, t_), 0)
    qcol = lax.broadcasted_iota(jnp.int32, (ATT_MASK_ROWS, t_), 1)
    qchunk = (qcol // PERM_BLOCK * PERM_BLOCK + _perm_token(qcol)) // CHUNK
    mask_rows = jnp.where((crow < t_ // CHUNK) & (crow > qchunk), NEG_BIG, 0.0).astype(BF16)
    for hh in range(nh):
        r0 = hh * hp + MLA_QK
        for qq in range(ATT_NQ):
            q2_ref[1, r0:r0 + ATT_MASK_ROWS, qq * t_:(qq + 1) * t_] = mask_rows
    p_ref[...] = jnp.zeros(p_ref.shape, BF16)
    acc_ref[...] = jnp.zeros(acc_ref.shape, F32)

    def nxt(qi, j):
        last = j >= qi
        return jnp.where(last, qi + 1, qi), jnp.where(last, 0, j + 1)

    def qk(slot, qi, j):
        diag = (j == qi).astype(jnp.int32)
        k0 = pl.multiple_of(j * t_, t_)
        q0 = pl.multiple_of(qi * t_, t_)
        for hh in range(nh):
            kt = k_ref[pl.ds(k0, t_), hh * hp:(hh + 1) * hp]
            qv = q2_ref[diag, hh * hp:(hh + 1) * hp, pl.ds(q0, t_)]
            st = _dot(kt, qv)
            s_ref[slot, hh] = st
            mt_ref[slot, hh] = jnp.max(st, axis=0, keepdims=True)

    def sm(slot, j, m):
        fresh = j == 0
        m_out, a_out = [], []
        for hh in range(nh):
            m_old = jnp.where(fresh, NEG_BIG, m[hh])
            m_new = jnp.maximum(m_old, mt_ref[slot, hh])
            m_out.append(m_new)
            a_out.append(jnp.exp2(m_old - m_new))
            p_ref[slot, hh] = jnp.exp2(s_ref[slot, hh] - m_new).astype(BF16)
        return tuple(m_out), tuple(a_out)

    def pv(slot, qi, j, alpha):
        k0 = pl.multiple_of(j * t_, t_)
        q0 = pl.multiple_of(qi * t_, t_)
        for hh in range(nh):
            vt = vt_ref[0, hh * ATT_V_PAD:(hh + 1) * ATT_V_PAD, pl.ds(k0, t_)]
            upd = _dot(vt, p_ref[slot, hh])
            acc_old = jnp.where(j == 0, 0.0, acc_ref[hh])
            acc_new = acc_old * alpha[hh] + upd
            acc_ref[hh] = acc_new
            o_ref[0, hh * MLA_V:(hh + 1) * MLA_V, pl.ds(q0, t_)] = (
                acc_new[0:MLA_V, :] * (1.0 / acc_new[MLA_V:MLA_V + 1, :])).astype(BF16)

    def step(par, carry, do_qk=True, do_sm=True):
        qa, ja, qb, jb, qc, jc, m, al = carry
        pv(par, qc, jc, al)
        if do_sm:
            m2, al2 = sm(1 - par, jb, m)
        else:
            m2, al2 = m, al
        if do_qk:
            qk(par, qa, ja)
        qa2, ja2 = nxt(qa, ja)
        return (qa2, ja2, qa, ja, qb, jb, m2, al2)

    zero = jnp.int32(0)
    qk(0, zero, zero)
    row = lambda v: tuple(jnp.full((1, t_), v, F32) for _ in range(nh))
    carry = (jnp.int32(1), zero, zero, zero, zero, zero, row(NEG_BIG), row(1.0))
    n_items = ATT_NQ * (ATT_NQ + 1) // 2
    assert n_items % 2 == 0
    quads, rest = divmod(n_items - 2, 4)
    assert rest == 2
    carry = lax.fori_loop(0, quads, lambda u, cr: step(0, step(1, step(0, step(1, cr)))), carry)
    carry = step(0, step(1, carry))
    carry = step(1, carry)
    carry = step(0, carry, do_qk=False)
    step(1, carry, do_qk=False, do_sm=False)


def _attention(qt, k, vt):
    nh = ATT_HEADS_PER_STEP
    return pl.pallas_call(
        _attn_kernel,
        grid=(BATCH, MLA_HEADS // nh),
        in_specs=[
            pl.BlockSpec((1, nh * MLA_HEAD_PAD, SEQ), lambda b, h: (b, h, 0)),
            pl.BlockSpec((SEQ, nh * MLA_HEAD_PAD), lambda b, h: (b, h)),
            pl.BlockSpec((1, nh * ATT_V_PAD, SEQ), lambda b, h: (b, h, 0)),
        ],
        out_specs=pl.BlockSpec((1, nh * MLA_V, SEQ), lambda b, h: (b, h, 0)),
        out_shape=jax.ShapeDtypeStruct((BATCH, MLA_HEADS * MLA_V, SEQ), BF16),
        scratch_shapes=[
            pltpu.VMEM((2, nh * MLA_HEAD_PAD, SEQ), BF16),
            pltpu.VMEM((2, nh, ATT_T, ATT_T), F32),
            pltpu.VMEM((2, nh, ATT_T, ATT_T), BF16),
            pltpu.VMEM((nh, ATT_V_PAD, ATT_T), F32),
            pltpu.VMEM((2, nh, 1, ATT_T), F32),
        ],
        compiler_params=_params(("arbitrary", "arbitrary")),
        name="mla_attention",
    )(qt, k, vt)


PROJ_TM = 512


def _proj_ln_kernel(x_ref, at_ref, w_ref, g_ref, b_ref, o_ref):
    y = lax.dot_general(at_ref[0], w_ref[...], (((0,), (0,)), ((), ())), preferred_element_type=F32)
    z = DN_ALPHA * x_ref[...] + y
    o_ref[...] = _layer_norm(z, g_ref[...], b_ref[...])


def _proj_ln(x, a_t, w, g, b):
    tm = PROJ_TM
    nl = SEQ // tm
    kdim = a_t.shape[1]
    return pl.pallas_call(
        _proj_ln_kernel,
        grid=(BATCH, nl),
        in_specs=[
            pl.BlockSpec((tm, D_MODEL), lambda bi, j: (bi * nl + j, 0)),
            pl.BlockSpec((1, kdim, tm), lambda bi, j: (bi, 0, j)),
            _const_spec((kdim, D_MODEL)),
            _const_spec((1, D_MODEL)),
            _const_spec((1, D_MODEL)),
        ],
        out_specs=pl.BlockSpec((tm, D_MODEL), lambda bi, j: (bi * nl + j, 0)),
        out_shape=jax.ShapeDtypeStruct((TOKENS, D_MODEL), F32),
        compiler_params=_params(("arbitrary", "arbitrary")),
        name="proj_ln",
    )(x, a_t, w, g, b)


SSM_TM = 512
SSD_Q = 128
SSD_TM = 512
LOG2E = math.log2(math.e)


def _ssm_in_kernel(x_ref, wz_ref, wxbc_ref, wdt_ref, cw_ref, cb_ref, dtb_ref,
                   z_ref, xs_ref, b_ref, bt_ref, c_ref, dt_ref, dtt_ref, buf_ref, halo_ref):
    tm = SSM_TM
    blk = CONV_BLK
    nblk = SSM_CONV_DIM // blk
    nz = SSM_D_INNER // blk

    @pl.when(pl.program_id(1) == 0)
    def _():
        halo_ref[...] = jnp.zeros(halo_ref.shape, F32)

    xb = x_ref[...].astype(BF16)
    dt = jax.nn.softplus(_dot(xb, wdt_ref[...]) + dtb_ref[...])
    dt_ref[...] = dt
    dtt_ref[0] = dt.T[0:SSM_HEADS, :]

    def project(k):
        _conv_store(buf_ref.at[k % 4], _dot(xb, wxbc_ref[:, k * blk:(k + 1) * blk]), SSM_CONV, tm)

    def gate_proj(k):
        z_ref[:, k * blk:(k + 1) * blk] = _dot(xb, wz_ref[:, k * blk:(k + 1) * blk]).astype(BF16)

    def conv(k):
        c0 = k * blk
        v = _silu(jnp.concatenate(_conv_block(buf_ref.at[k % 4], halo_ref, cw_ref, cb_ref, c0, SSM_CONV, tm), axis=0))
        if c0 < SSM_D_INNER:
            xs_ref[:, c0:c0 + blk] = v.astype(BF16)
        elif c0 < SSM_D_INNER + SSM_BC:
            n0 = c0 - SSM_D_INNER
            b_ref[:, n0:n0 + blk] = v.astype(BF16)
            bt_ref[0, n0:n0 + blk, :] = v.T.astype(BF16)
        else:
            n0 = c0 - SSM_D_INNER - SSM_BC
            c_ref[:, n0:n0 + blk] = v.astype(BF16)

    assert nblk == 2 * nz
    project(0)
    project(1)
    for k in range(0, nblk, 2):
        if k + 2 < nblk:
            project(k + 2)
            project(k + 3)
        gate_proj(k // 2)
        conv(k)
        conv(k + 1)


def _ssm_in(x, w_z, w_xbc, w_dt, conv_w, conv_b, dt_bias):
    tm = SSM_TM
    nl = SEQ // tm
    tok = lambda n: pl.BlockSpec((tm, n), lambda b, j: (b * nl + j, 0))
    return pl.pallas_call(
        _ssm_in_kernel,
        grid=(BATCH, nl),
        in_specs=[
            tok(D_MODEL),
            _const_spec((D_MODEL, SSM_D_INNER)),
            _const_spec((D_MODEL, SSM_CONV_DIM)),
            _const_spec((D_MODEL, LANES)),
            _const_spec((SSM_CONV, SSM_CONV_DIM)),
            _const_spec((1, SSM_CONV_DIM)),
            _const_spec((1, LANES)),
        ],
        out_specs=[
            tok(SSM_D_INNER),
            tok(SSM_D_INNER),
            tok(SSM_BC),
            pl.BlockSpec((1, SSM_BC, tm), lambda b, j: (b, 0, j)),
            tok(SSM_BC),
            tok(LANES),
            pl.BlockSpec((1, SSM_HEADS, tm), lambda b, j: (b, 0, j)),
        ],
        out_shape=[
            jax.ShapeDtypeStruct((TOKENS, SSM_D_INNER), BF16),
            jax.ShapeDtypeStruct((TOKENS, SSM_D_INNER), BF16),
            jax.ShapeDtypeStruct((TOKENS, SSM_BC), BF16),
            jax.ShapeDtypeStruct((BATCH, SSM_BC, SEQ), BF16),
            jax.ShapeDtypeStruct((TOKENS, SSM_BC), BF16),
            jax.ShapeDtypeStruct((TOKENS, LANES), F32),
            jax.ShapeDtypeStruct((BATCH, SSM_HEADS, SEQ), F32),
        ],
        scratch_shapes=[
            pltpu.VMEM((4, _conv_rows(SSM_CONV, tm), CONV_BLK), F32),
            pltpu.VMEM(((SSM_CONV - 1) * SUBLANES, SSM_CONV_DIM), F32),
        ],
        compiler_params=_params(("arbitrary", "arbitrary")),
        name="ssm_in",
    )(x, w_z, w_xbc, w_dt, conv_w, conv_b, dt_bias)


def _ssd_kernel(x_ref, z_ref, xs_ref, b_ref, bt_ref, c_ref, dt_ref, dtt_ref,
                alog_row_ref, alog_col_ref, dskip_ref, ng_ref, rexp_ref, wout_ref, g_ref, bb_ref,
                o_ref, state_ref, y_ref, xbd_ref, yb_ref):
    q = SSD_Q
    hi = lax.Precision.HIGHEST
    hw = SSM_HPG * SSM_HEAD_DIM

    @pl.when(pl.program_id(1) == 0)
    def _():
        state_ref[...] = jnp.zeros(state_ref.shape, F32)
        xbd_ref[...] = jnp.zeros(xbd_ref.shape, BF16)

    row = lax.broadcasted_iota(jnp.int32, (q, q), 0)
    col = lax.broadcasted_iota(jnp.int32, (q, q), 1)
    lower = _perm_token(col) <= _perm_token(row)
    tri = lower.astype(F32)
    a_row = -jnp.exp(alog_row_ref[...]) * LOG2E
    a_col = -jnp.exp(alog_col_ref[...]) * LOG2E
    lane_head = lax.broadcasted_iota(jnp.int32, (1, hw), 1) // SSM_HEAD_DIM

    def scan(c):
        rows = slice(c * q, (c + 1) * q)
        dt = dt_ref[rows, :]
        dt_t = dtt_ref[0, :, rows]
        acum = jnp.dot(tri, dt * a_row, precision=hi, preferred_element_type=F32)
        acum_t = lax.dot_general(dt_t * a_col, tri, (((1,), (1,)), ((), ())), precision=hi,
                                 preferred_element_type=F32)
        e_col = jnp.exp2(acum).astype(BF16)
        w_col = (jnp.exp2(acum[q - 1:q, :] - acum) * dt).astype(BF16)
        rowterm = acum_t - jnp.log2(dt_t)
        cd = jnp.exp2(acum_t[:, q - 1:q])
        for grp in range(SSM_GROUPS):
            n0 = grp * SSM_STATE
            c0 = grp * hw
            cg = c_ref[rows, n0:n0 + SSM_STATE]
            cb = _dot_nt(cg, b_ref[rows, n0:n0 + SSM_STATE])
            rg = rexp_ref[:, c0:c0 + hw]
            eg = _dot(e_col, rg)
            wg = _dot(w_col, rg)
            ms = []
            cdg = jnp.zeros((1, hw), F32)
            for r in range(SSM_HPG):
                h = grp * SSM_HPG + r
                p0 = c0 + r * SSM_HEAD_DIM
                seg = acum[:, h:h + 1] - rowterm[h:h + 1, :]
                ms.append((cb * jnp.exp2(jnp.where(lower, seg, NEG_BIG))).astype(BF16))
                xbd_ref[c, grp, r * q:(r + 1) * q, r * SSM_HEAD_DIM:(r + 1) * SSM_HEAD_DIM] = xs_ref[rows, p0:p0 + SSM_HEAD_DIM]
                cdg = jnp.where(lane_head == r, cd[h:h + 1, :], cdg)
            s_old = state_ref[grp]
            xgf = xs_ref[rows, c0:c0 + hw].astype(F32)
            yg = (_dot(jnp.concatenate(ms, axis=1), xbd_ref[c, grp])
                  + _dot(cg, s_old.astype(BF16)) * eg + dskip_ref[:, c0:c0 + hw] * xgf)
            y_ref[rows, c0:c0 + hw] = yg
            state_ref[grp] = s_old * cdg + _dot(bt_ref[0, n0:n0 + SSM_STATE, rows], (xgf * wg).astype(BF16))

    def finish(c):
        rows = slice(c * q, (c + 1) * q)
        y = y_ref[rows, :] * _silu(z_ref[rows, :].astype(F32))
        for grp in range(SSM_GROUPS):
            c0 = grp * hw
            yb_ref[rows, c0:c0 + hw] = _rms(y[:, c0:c0 + hw], ng_ref[:, c0:c0 + hw]).astype(BF16)
        zres = DN_ALPHA * x_ref[rows, :] + _dot(yb_ref[rows, :], wout_ref[...])
        o_ref[rows, :] = _layer_norm(zres, g_ref[...], bb_ref[...])

    nc = SSD_TM // q
    for c in range(nc + 1):
        if c < nc:
            scan(c)
        if c >= 1:
            finish(c - 1)


def _ssd(x, z, xs, bm, bt, cm, dt, dtt, alog_row, alog_col, dskip, ng, rexp, w_out, g, b):
    tm = SSD_TM
    nl = SEQ // tm
    tok = lambda n: pl.BlockSpec((tm, n), lambda bi, j: (bi * nl + j, 0))
    return pl.pallas_call(
        _ssd_kernel,
        grid=(BATCH, nl),
        in_specs=[
            tok(D_MODEL),
            tok(SSM_D_INNER),
            tok(SSM_D_INNER),
            tok(SSM_BC),
            pl.BlockSpec((1, SSM_BC, tm), lambda bi, j: (bi, 0, j)),
            tok(SSM_BC),
            tok(LANES),
            pl.BlockSpec((1, SSM_HEADS, tm), lambda bi, j: (bi, 0, j)),
            _const_spec((1, LANES)),
            _const_spec((SSM_HEADS, 1)),
            _const_spec((1, SSM_D_INNER)),
            _const_spec((1, SSM_D_INNER)),
            _const_spec((LANES, SSM_D_INNER)),
            _const_spec((SSM_D_INNER, D_MODEL)),
            _const_spec((1, D_MODEL)),
            _const_spec((1, D_MODEL)),
        ],
        out_specs=tok(D_MODEL),
        out_shape=jax.ShapeDtypeStruct((TOKENS, D_MODEL), F32),
        scratch_shapes=[
            pltpu.VMEM((SSM_GROUPS, SSM_STATE, SSM_HPG * SSM_HEAD_DIM), F32),
            pltpu.VMEM((tm, SSM_D_INNER), F32),
            pltpu.VMEM((tm // SSD_Q, SSM_GROUPS, SSM_HPG * SSD_Q, SSM_HPG * SSM_HEAD_DIM), BF16),
            pltpu.VMEM((tm, SSM_D_INNER), BF16),
        ],
        compiler_params=_params(("arbitrary", "arbitrary")),
        name="ssd_scan",
    )(x, z, xs, bm, bt, cm, dt, dtt, alog_row, alog_col, dskip, ng, rexp, w_out, g, b)


def _row(v):
    return v.reshape(1, -1).astype(F32)


def _mamba_layer(x, w_in, conv_w, conv_b, dt_bias, a_log, d_skip, norm_g, w_out, g, b):
    w_z = w_in[:, :SSM_D_INNER].astype(BF16)
    w_xbc = w_in[:, SSM_D_INNER:SSM_D_INNER + SSM_CONV_DIM].astype(BF16)
    w_dt = jnp.pad(w_in[:, SSM_D_INNER + SSM_CONV_DIM:], ((0, 0), (0, LANES - SSM_HEADS))).astype(BF16)
    dtb = jnp.pad(dt_bias, (0, LANES - SSM_HEADS)).reshape(1, LANES)
    z, xs, bm, bt, cm, dt, dtt = _ssm_in(x, w_z, w_xbc, w_dt, conv_w, _row(conv_b), dtb)
    alog_row = jnp.pad(a_log, (0, LANES - SSM_HEADS)).reshape(1, LANES)
    alog_col = a_log.reshape(SSM_HEADS, 1)
    dskip = jnp.repeat(d_skip, SSM_HEAD_DIM).reshape(1, SSM_D_INNER)
    rexp = jnp.repeat(jnp.eye(LANES, SSM_HEADS, dtype=F32), SSM_HEAD_DIM, axis=1).astype(BF16)
    return _ssd(x, z, xs, bm, bt, cm, dt, dtt, alog_row, alog_col, dskip, _row(norm_g), rexp,
                w_out.astype(BF16), _row(g), _row(b))


def _sg_layer(x, w_in, b_in, ln_g, ln_b, w_s, b_s, w_out, g, b):
    tok = _perm_token(jnp.arange(SG_BLOCK))
    b_s_wide = jnp.repeat(b_s.T[tok], SG_GROUP_DIM, axis=1)
    w_s = w_s[:, tok][:, :, tok]
    return _spatial_gating(x, w_in.astype(BF16), _row(b_in), _row(ln_g), _row(ln_b), w_s, b_s_wide,
                           w_out.astype(BF16), _row(g), _row(b))


def _mla_layer(x, positions, w_in, q_norm_g, w_q_b, kv_norm_g, w_kv_b, w_out, g, b):
    h = MLA_HEADS
    s0 = MLA_Q_RANK + MLA_KV_RANK
    t1 = w_in[:, s0:s0 + MLA_HALF]
    t2 = w_in[:, s0 + MLA_HALF:s0 + MLA_ROPE]
    zl = jnp.zeros((D_MODEL, MLA_NOPE), F32)
    zh = jnp.zeros((D_MODEL, LANES - MLA_QK), F32)
    w_in_pad = jnp.concatenate([w_in[:, :s0], zl, t1, t2, zh, zl, -t2, t1, zh], axis=1).astype(BF16)

    wq = w_q_b.reshape(MLA_Q_RANK, h, MLA_QK)
    wq = jnp.pad(wq, ((0, 0), (0, 0), (0, MLA_HEAD_PAD - MLA_QK)))
    wqt = wq.reshape(MLA_Q_RANK, h * MLA_HEAD_PAD).T.astype(BF16)
    wkv = w_kv_b.reshape(MLA_KV_RANK, h, MLA_NOPE + MLA_V)
    wk = jnp.pad(wkv[:, :, :MLA_NOPE], ((0, 0), (0, 0), (0, MLA_HEAD_PAD - MLA_NOPE)))
    wk = wk.reshape(MLA_KV_RANK, h * MLA_HEAD_PAD).astype(BF16)
    wvt = wkv[:, :, MLA_NOPE:].reshape(MLA_KV_RANK, h * MLA_V).T.astype(BF16)

    inv = (ROPE_THETA ** (-(jnp.arange(MLA_HALF, dtype=F32) * 2.0 / MLA_ROPE))).reshape(MLA_HALF, 1)
    cos_t, sin_t, cos_tab, sin_tab = _rope_tables(positions, inv)
    qt, k, vt = _mla_proj(x, w_in_pad, _row(q_norm_g), _row(kv_norm_g), wqt, wk, wvt,
                          cos_t, sin_t, cos_tab, sin_tab)
    o = _attention(qt, k, vt)
    return _proj_ln(x, o, w_out.astype(BF16), _row(g), _row(b))


def kernel(x, positions, ssm_w_in, ssm_conv_w, ssm_conv_b, ssm_dt_bias, ssm_a_log, ssm_d, ssm_norm_g, ssm_w_out, sg_w_in, sg_b_in, sg_ln_g, sg_ln_b, sg_w_s, sg_b_s, sg_w_out, mla_w_in, mla_q_norm_g, mla_w_q_b, mla_kv_norm_g, mla_w_kv_b, mla_w_out, ffn_w_in, ffn_conv_w, ffn_conv_b, ffn_w_out, ln_g, ln_b):
    h = _permute_tokens(x).reshape(TOKENS, D_MODEL)
    positions = _permute_tokens(positions)
    for i in range(DEPTH):
        m, j = i % N_MIXERS, i // N_MIXERS
        g0, b0 = ln_g[i, 0], ln_b[i, 0]
        if m == 0:
            h = _mamba_layer(h, ssm_w_in[j], ssm_conv_w[j], ssm_conv_b[j], ssm_dt_bias[j], ssm_a_log[j],
                             ssm_d[j], ssm_norm_g[j], ssm_w_out[j], g0, b0)
        elif m == 1:
            h = _sg_layer(h, sg_w_in[j], sg_b_in[j], sg_ln_g[j], sg_ln_b[j], sg_w_s[j], sg_b_s[j],
                          sg_w_out[j], g0, b0)
        else:
            h = _mla_layer(h, positions, mla_w_in[j], mla_q_norm_g[j], mla_w_q_b[j], mla_kv_norm_g[j],
                           mla_w_kv_b[j], mla_w_out[j], g0, b0)
        h = _ffn(h, ffn_w_in[i].astype(BF16), ffn_conv_w[i], _row(ffn_conv_b[i]),
                 ffn_w_out[i].astype(BF16), _row(ln_g[i, 1]), _row(ln_b[i, 1]))
    return _unpermute_tokens(h.reshape(BATCH, SEQ, D_MODEL))
```

```python
import functools
import math

import jax
import jax.numpy as jnp
from jax import lax
from jax.experimental import pallas as pl
from jax.experimental.pallas import tpu as pltpu

D_MODEL = 1024
BATCH = 8
SEQ = 4096
DEPTH = 4
TOKENS = BATCH * SEQ

CHUNK = 64
N_MIXERS = 3

DN_ALPHA = (2.0 * DEPTH) ** 0.25
LN_EPS = 1e-5
RMS_EPS = 1e-6

SSM_D_INNER = 2048
SSM_HEAD_DIM = 64
SSM_HEADS = 32
SSM_GROUPS = 8
SSM_HPG = 4
SSM_STATE = 128
SSM_CONV = 4
SSM_CONV_DIM = 4096
SSM_BC = SSM_GROUPS * SSM_STATE

SG_BLOCK = 128
SG_WIDTH = 2048
SG_GROUPS = 8
SG_GROUP_DIM = 256

MLA_HEADS = 16
MLA_Q_RANK = 384
MLA_KV_RANK = 256
MLA_NOPE = 64
MLA_ROPE = 32
MLA_HALF = 16
MLA_V = 64
MLA_QK = MLA_NOPE + MLA_ROPE
ROPE_THETA = 10000.0

FFN_HIDDEN = 2816
FFN_CONV = 3

LANES = 128
SUBLANES = 8
PERM_BLOCK = 128
PERM_GROUPS = PERM_BLOCK // SUBLANES
VMEM_LIMIT = 56 * 1024 * 1024

BF16 = jnp.bfloat16
F32 = jnp.float32
NEG_BIG = -1e30


def _dot(a, b):
    return jnp.dot(a, b, preferred_element_type=F32)


def _dot_nt(a, b):
    return lax.dot_general(a, b, (((1,), (1,)), ((), ())), preferred_element_type=F32)


def _layer_norm(z, g, b):
    mu = jnp.mean(z, axis=-1, keepdims=True)
    zc = z - mu
    var = jnp.mean(zc * zc, axis=-1, keepdims=True)
    return zc * lax.rsqrt(var + LN_EPS) * g + b


def _silu(x):
    return x * (1.0 / (1.0 + jnp.exp(-x)))


def _const_spec(shape):
    nd = len(shape)
    return pl.BlockSpec(shape, lambda *_: (0,) * nd, pipeline_mode=pl.Buffered(1))


def _params(sem):
    return pltpu.CompilerParams(dimension_semantics=sem, vmem_limit_bytes=VMEM_LIMIT)


FFN_TM = 1024
CONV_BLK = 2 * LANES
FFN_NBLK = FFN_HIDDEN // CONV_BLK
FFN_OUT_GROUP = 3
FFN_NGROUP = -(-FFN_NBLK // FFN_OUT_GROUP)


def _permute_tokens(a):
    shp = a.shape
    a = a.reshape(shp[0], shp[1] // PERM_BLOCK, SUBLANES, PERM_GROUPS, *shp[2:])
    return jnp.swapaxes(a, 2, 3).reshape(shp)


def _unpermute_tokens(a):
    shp = a.shape
    a = a.reshape(shp[0], shp[1] // PERM_BLOCK, PERM_GROUPS, SUBLANES, *shp[2:])
    return jnp.swapaxes(a, 2, 3).reshape(shp)


def _perm_token(idx):
    return (idx % SUBLANES) * PERM_GROUPS + (idx % PERM_BLOCK) // SUBLANES


def _conv_rows(taps, tm):
    return (tm // PERM_BLOCK) * (PERM_BLOCK + (taps - 1) * SUBLANES)


def _conv_store(buf, res, taps, tm):
    pad = (taps - 1) * SUBLANES
    stride = PERM_BLOCK + pad
    for blk in range(tm // PERM_BLOCK):
        buf[blk * stride + pad:(blk + 1) * stride, :] = res[blk * PERM_BLOCK:(blk + 1) * PERM_BLOCK, :]


def _conv_block(buf, halo_ref, cw_ref, cb_ref, c0, taps, tm):
    pad = (taps - 1) * SUBLANES
    stride = PERM_BLOCK + pad
    nb = tm // PERM_BLOCK
    top = lax.broadcasted_iota(jnp.int32, (SUBLANES, CONV_BLK), 0) == 0
    outs = []
    for blk in range(nb):
        base = blk * stride
        for s in range(1, taps):
            g0 = base + stride - s * SUBLANES
            cur = buf[g0:g0 + SUBLANES, :]
            if blk == 0:
                prev = halo_ref[(s - 1) * SUBLANES:s * SUBLANES, c0:c0 + CONV_BLK]
            else:
                prev = buf[base - s * SUBLANES:base - (s - 1) * SUBLANES, :]
            v0 = base + pad - s * SUBLANES
            buf[v0:v0 + SUBLANES, :] = jnp.where(top, pltpu.roll(prev, 1, 0), pltpu.roll(cur, 1, 0))
        acc = cb_ref[:, c0:c0 + CONV_BLK]
        for t in range(taps):
            r0 = base + t * SUBLANES
            acc = acc + cw_ref[t:t + 1, c0:c0 + CONV_BLK] * buf[r0:r0 + PERM_BLOCK, :]
        outs.append(acc)
    for s in range(1, taps):
        g0 = nb * stride - s * SUBLANES
        halo_ref[(s - 1) * SUBLANES:s * SUBLANES, c0:c0 + CONV_BLK] = buf[g0:g0 + SUBLANES, :]
    return outs


def _ffn_kernel(x_ref, win_ref, cw_ref, cb_ref, wout_ref, g_ref, b_ref, o_ref,
                buf_ref, halo_ref, act_ref, acc_ref, xb_ref):
    tm = FFN_TM

    @pl.when(pl.program_id(0) % (SEQ // tm) == 0)
    def _():
        halo_ref[...] = jnp.zeros(halo_ref.shape, F32)

    xb_ref[...] = x_ref[...].astype(BF16)

    def project(k):
        for half in range(2):
            off = half * FFN_HIDDEN + k * CONV_BLK
            _conv_store(buf_ref.at[k % 2, half], _dot(xb_ref[...], win_ref[:, off:off + CONV_BLK]), FFN_CONV, tm)

    def gate(k):
        conv = [_conv_block(buf_ref.at[k % 2, half], halo_ref, cw_ref, cb_ref,
                            half * FFN_HIDDEN + k * CONV_BLK, FFN_CONV, tm) for half in range(2)]
        j = k % FFN_OUT_GROUP
        for blk in range(tm // PERM_BLOCK):
            act_ref[k // FFN_OUT_GROUP, blk * PERM_BLOCK:(blk + 1) * PERM_BLOCK, j * CONV_BLK:(j + 1) * CONV_BLK] = (
                _silu(conv[0][blk]) * conv[1][blk]).astype(BF16)

    def contract(grp):
        k0 = grp * FFN_OUT_GROUP
        n = min(FFN_OUT_GROUP, FFN_NBLK - k0) * CONV_BLK
        part = _dot(act_ref[grp, :, 0:n], wout_ref[k0 * CONV_BLK:k0 * CONV_BLK + n, :])
        if grp == 0:
            acc_ref[...] = part
        else:
            acc_ref[...] += part

    project(0)
    for k in range(FFN_NBLK):
        if k + 1 < FFN_NBLK:
            project(k + 1)
        gate(k)
        if (k + 1) % FFN_OUT_GROUP == 0 or k + 1 == FFN_NBLK:
            contract(k // FFN_OUT_GROUP)
    z = DN_ALPHA * x_ref[...] + acc_ref[...]
    o_ref[...] = _layer_norm(z, g_ref[...], b_ref[...])


def _ffn(x, w_in, conv_w, conv_b, w_out, g, b):
    tm = FFN_TM
    return pl.pallas_call(
        _ffn_kernel,
        grid=(TOKENS // tm,),
        in_specs=[
            pl.BlockSpec((tm, D_MODEL), lambda i: (i, 0)),
            _const_spec((D_MODEL, 2 * FFN_HIDDEN)),
            _const_spec((FFN_CONV, 2 * FFN_HIDDEN)),
            _const_spec((1, 2 * FFN_HIDDEN)),
            _const_spec((FFN_HIDDEN, D_MODEL)),
            _const_spec((1, D_MODEL)),
            _const_spec((1, D_MODEL)),
        ],
        out_specs=pl.BlockSpec((tm, D_MODEL), lambda i: (i, 0)),
        out_shape=jax.ShapeDtypeStruct((TOKENS, D_MODEL), F32),
        scratch_shapes=[
            pltpu.VMEM((2, 2, _conv_rows(FFN_CONV, tm), CONV_BLK), F32),
            pltpu.VMEM(((FFN_CONV - 1) * SUBLANES, 2 * FFN_HIDDEN), F32),
            pltpu.VMEM((FFN_NGROUP, tm, FFN_OUT_GROUP * CONV_BLK), BF16),
            pltpu.VMEM((tm, D_MODEL), F32),
            pltpu.VMEM((tm, D_MODEL), BF16),
        ],
        compiler_params=_params(("arbitrary",)),
        name="conv_ffn",
    )(x, w_in, conv_w, conv_b, w_out, g, b)


SG_TM = 512
SG_PROJ_BLK = 4 * LANES


def _sg_kernel(x_ref, win_ref, bin_ref, lng_ref, lnb_ref, ws_ref, bs_ref, wout_ref, g_ref, b_ref,
               o_ref, v32_ref, v_ref, gated_ref, h_ref, xb_ref, acc_ref):
    tm = SG_TM
    xb_ref[...] = x_ref[...].astype(BF16)
    nhalf = SG_WIDTH // SG_PROJ_BLK
    gpb = SG_PROJ_BLK // SG_GROUP_DIM

    def project(k):
        c0 = k * SG_PROJ_BLK
        h_ref[k % 2] = _dot(xb_ref[...], win_ref[:, c0:c0 + SG_PROJ_BLK]) + bin_ref[:, c0:c0 + SG_PROJ_BLK]

    project(nhalf)
    for k in range(nhalf, 2 * nhalf):
        project(k + 1 if k + 1 < 2 * nhalf else 0)
        c0 = (k - nhalf) * SG_PROJ_BLK
        v32_ref[:, c0:c0 + SG_PROJ_BLK] = jax.nn.gelu(h_ref[k % 2])
    v_ref[...] = _layer_norm(v32_ref[...], lng_ref[...], lnb_ref[...]).astype(BF16)

    row = lax.broadcasted_iota(jnp.int32, (SG_BLOCK, SG_BLOCK), 0)
    col = lax.broadcasted_iota(jnp.int32, (SG_BLOCK, SG_BLOCK), 1)
    causal = _perm_token(col) <= _perm_token(row)
    for k in range(nhalf):
        if k + 1 < nhalf:
            project(k + 1)
        u = jax.nn.gelu(h_ref[k % 2])
        for gi in range(gpb):
            grp = k * gpb + gi
            c0 = grp * SG_GROUP_DIM
            ws = jnp.where(causal, ws_ref[grp], 0.0).astype(BF16)
            bias = bs_ref[:, c0:c0 + SG_GROUP_DIM]
            for blk in range(tm // SG_BLOCK):
                r0 = blk * SG_BLOCK
                mixed = _dot(ws, v_ref[r0:r0 + SG_BLOCK, c0:c0 + SG_GROUP_DIM]) + bias
                gated_ref[k, r0:r0 + SG_BLOCK, gi * SG_GROUP_DIM:(gi + 1) * SG_GROUP_DIM] = (
                    u[r0:r0 + SG_BLOCK, gi * SG_GROUP_DIM:(gi + 1) * SG_GROUP_DIM] * mixed).astype(BF16)
        part = _dot(gated_ref[k], wout_ref[k * SG_PROJ_BLK:(k + 1) * SG_PROJ_BLK, :])
        if k == 0:
            acc_ref[...] = part
        else:
            acc_ref[...] += part
    z = DN_ALPHA * x_ref[...] + acc_ref[...]
    o_ref[...] = _layer_norm(z, g_ref[...], b_ref[...])


def _spatial_gating(x, w_in, b_in, ln_g, ln_b, w_s, b_s_wide, w_out, g, b):
    tm = SG_TM
    return pl.pallas_call(
        _sg_kernel,
        grid=(TOKENS // tm,),
        in_specs=[
            pl.BlockSpec((tm, D_MODEL), lambda i: (i, 0)),
            _const_spec((D_MODEL, 2 * SG_WIDTH)),
            _const_spec((1, 2 * SG_WIDTH)),
            _const_spec((1, SG_WIDTH)),
            _const_spec((1, SG_WIDTH)),
            _const_spec((SG_GROUPS, SG_BLOCK, SG_BLOCK)),
            _const_spec((SG_BLOCK, SG_WIDTH)),
            _const_spec((SG_WIDTH, D_MODEL)),
            _const_spec((1, D_MODEL)),
            _const_spec((1, D_MODEL)),
        ],
        out_specs=pl.BlockSpec((tm, D_MODEL), lambda i: (i, 0)),
        out_shape=jax.ShapeDtypeStruct((TOKENS, D_MODEL), F32),
        scratch_shapes=[
            pltpu.VMEM((tm, SG_WIDTH), F32),
            pltpu.VMEM((tm, SG_WIDTH), BF16),
            pltpu.VMEM((SG_WIDTH // SG_PROJ_BLK, tm, SG_PROJ_BLK), BF16),
            pltpu.VMEM((2, tm, SG_PROJ_BLK), F32),
            pltpu.VMEM((tm, D_MODEL), BF16),
            pltpu.VMEM((tm, D_MODEL), F32),
        ],
        compiler_params=_params(("arbitrary",)),
        name="spatial_gating",
    )(x, w_in, b_in, ln_g, ln_b, w_s, b_s_wide, w_out, g, b)


ROPE_TL = 512
MLA_TM = 512
MLA_LAT = MLA_Q_RANK + MLA_KV_RANK + 2 * LANES
MLA_HEAD_PAD = LANES
ATT_T = 256
ATT_NQ = SEQ // ATT_T
ATT_HEADS_PER_STEP = 4
ATT_MASK_ROWS = 16
ATT_Q_SCALE = (MLA_QK ** -0.5) * math.log2(math.e)
ATT_V_PAD = 80


def _rope_kernel(pos_ref, inv_ref, cos_t_ref, sin_t_ref, cos_tab_ref, sin_tab_ref):
    ang = inv_ref[...] * pos_ref[0].astype(F32)
    c = jnp.cos(ang)
    s = jnp.sin(ang)
    cos_t_ref[0] = c
    sin_t_ref[0] = s
    zlo = jnp.zeros((MLA_NOPE, ROPE_TL), F32)
    zhi = jnp.zeros((LANES - MLA_QK, ROPE_TL), F32)
    cos_tab_ref[...] = jnp.concatenate([zlo, c, c, zhi], axis=0).T
    sin_tab_ref[...] = jnp.concatenate([zlo, s, s, zhi], axis=0).T


def _rope_tables(positions, inv):
    nl = SEQ // ROPE_TL
    pos3 = positions.reshape(BATCH, 1, SEQ)
    return pl.pallas_call(
        _rope_kernel,
        grid=(BATCH, nl),
        in_specs=[
            pl.BlockSpec((1, 1, ROPE_TL), lambda b, j: (b, 0, j)),
            pl.BlockSpec((MLA_HALF, 1), lambda b, j: (0, 0)),
        ],
        out_specs=[
            pl.BlockSpec((1, MLA_HALF, ROPE_TL), lambda b, j: (b, 0, j)),
            pl.BlockSpec((1, MLA_HALF, ROPE_TL), lambda b, j: (b, 0, j)),
            pl.BlockSpec((ROPE_TL, LANES), lambda b, j: (b * nl + j, 0)),
            pl.BlockSpec((ROPE_TL, LANES), lambda b, j: (b * nl + j, 0)),
        ],
        out_shape=[
            jax.ShapeDtypeStruct((BATCH, MLA_HALF, SEQ), F32),
            jax.ShapeDtypeStruct((BATCH, MLA_HALF, SEQ), F32),
            jax.ShapeDtypeStruct((TOKENS, LANES), F32),
            jax.ShapeDtypeStruct((TOKENS, LANES), F32),
        ],
        compiler_params=_params(("arbitrary", "arbitrary")),
        name="rope_tables",
    )(pos3, inv)


def _rms(x, g):
    return x * lax.rsqrt(jnp.mean(x * x, axis=-1, keepdims=True) + RMS_EPS) * g


def _mla_proj_kernel(x_ref, win_ref, gq_ref, gkv_ref, wqt_ref, wk_ref, wvt_ref,
                     cos_t_ref, sin_t_ref, cos_tab_ref, sin_tab_ref,
                     qt_ref, k_ref, vt_ref):
    xb = x_ref[...].astype(BF16)
    lat = _dot(xb, win_ref[...])
    qn = _rms(lat[:, 0:MLA_Q_RANK], gq_ref[...]).astype(BF16)
    kvn = _rms(lat[:, MLA_Q_RANK:MLA_Q_RANK + MLA_KV_RANK], gkv_ref[...]).astype(BF16)
    s0 = MLA_Q_RANK + MLA_KV_RANK
    kslab = lat[:, s0:s0 + LANES] * cos_tab_ref[...] + lat[:, s0 + LANES:s0 + 2 * LANES] * sin_tab_ref[...]
    lane = lax.broadcasted_iota(jnp.int32, (MLA_TM, LANES), 1) - MLA_QK
    rowi = lax.broadcasted_iota(jnp.int32, (MLA_TM, LANES), 0)
    chunk = ((rowi % ATT_T) // PERM_BLOCK * PERM_BLOCK + _perm_token(rowi)) // CHUNK
    kslab = kslab + jnp.where(lane == chunk, 1.0, 0.0)
    kmat = _dot(kvn, wk_ref[...])
    for h in range(MLA_HEADS):
        c0 = h * MLA_HEAD_PAD
        k_ref[:, c0:c0 + MLA_HEAD_PAD] = (kmat[:, c0:c0 + MLA_HEAD_PAD] + kslab).astype(BF16)
    vt = _dot_nt(wvt_ref[...], kvn)
    ones_row = (lax.broadcasted_iota(jnp.int32, (ATT_V_PAD - MLA_V, MLA_TM), 0) == 0).astype(BF16)
    for h in range(MLA_HEADS):
        vt_ref[0, h * ATT_V_PAD:h * ATT_V_PAD + MLA_V, :] = vt[h * MLA_V:(h + 1) * MLA_V, :].astype(BF16)
        vt_ref[0, h * ATT_V_PAD + MLA_V:(h + 1) * ATT_V_PAD, :] = ones_row
    qt = _dot_nt(wqt_ref[...], qn) * ATT_Q_SCALE
    cos_t = cos_t_ref[0]
    sin_t = sin_t_ref[0]
    for h in range(MLA_HEADS):
        r0 = h * MLA_HEAD_PAD
        qt_ref[0, r0:r0 + MLA_NOPE, :] = qt[r0:r0 + MLA_NOPE, :].astype(BF16)
        t1 = qt[r0 + MLA_NOPE:r0 + MLA_NOPE + MLA_HALF, :]
        t2 = qt[r0 + MLA_NOPE + MLA_HALF:r0 + MLA_QK, :]
        qt_ref[0, r0 + MLA_NOPE:r0 + MLA_NOPE + MLA_HALF, :] = (t1 * cos_t - t2 * sin_t).astype(BF16)
        qt_ref[0, r0 + MLA_NOPE + MLA_HALF:r0 + MLA_QK, :] = (t2 * cos_t + t1 * sin_t).astype(BF16)
        qt_ref[0, r0 + MLA_QK:r0 + MLA_HEAD_PAD, :] = jnp.zeros((MLA_HEAD_PAD - MLA_QK, MLA_TM), BF16)


def _mla_proj(x, w_in_pad, gq, gkv, wqt, wk, wvt, cos_t, sin_t, cos_tab, sin_tab):
    tm = MLA_TM
    nl = SEQ // tm
    hp = MLA_HEADS * MLA_HEAD_PAD
    return pl.pallas_call(
        _mla_proj_kernel,
        grid=(BATCH, nl),
        in_specs=[
            pl.BlockSpec((tm, D_MODEL), lambda b, j: (b * nl + j, 0)),
            _const_spec((D_MODEL, MLA_LAT)),
            _const_spec((1, MLA_Q_RANK)),
            _const_spec((1, MLA_KV_RANK)),
            _const_spec((hp, MLA_Q_RANK)),
            _const_spec((MLA_KV_RANK, hp)),
            _const_spec((MLA_HEADS * MLA_V, MLA_KV_RANK)),
            pl.BlockSpec((1, MLA_HALF, tm), lambda b, j: (b, 0, j)),
            pl.BlockSpec((1, MLA_HALF, tm), lambda b, j: (b, 0, j)),
            pl.BlockSpec((tm, LANES), lambda b, j: (b * nl + j, 0)),
            pl.BlockSpec((tm, LANES), lambda b, j: (b * nl + j, 0)),
        ],
        out_specs=[
            pl.BlockSpec((1, hp, tm), lambda b, j: (b, 0, j)),
            pl.BlockSpec((tm, hp), lambda b, j: (b * nl + j, 0)),
            pl.BlockSpec((1, MLA_HEADS * ATT_V_PAD, tm), lambda b, j: (b, 0, j)),
        ],
        out_shape=[
            jax.ShapeDtypeStruct((BATCH, hp, SEQ), BF16),
            jax.ShapeDtypeStruct((TOKENS, hp), BF16),
            jax.ShapeDtypeStruct((BATCH, MLA_HEADS * ATT_V_PAD, SEQ), BF16),
        ],
        compiler_params=_params(("arbitrary", "arbitrary")),
        name="mla_proj",
    )(x, w_in_pad, gq, gkv, wqt, wk, wvt, cos_t, sin_t, cos_tab, sin_tab)


def _attn_kernel(qt_ref, k_ref, vt_ref, o_ref, q2_ref, s_ref, p_ref, acc_ref, mt_ref):
    t_ = ATT_T
    nh = ATT_HEADS_PER_STEP
    hp = MLA_HEAD_PAD

    q2_ref[0] = qt_ref[0]
    q2_ref[1] = qt_ref[0]
    crow = lax.broadcasted_iota(jnp.int32, (ATT_MASK_ROWS, t_), 0)
    qcol = lax.broadcasted_iota(jnp.int32, (ATT_MASK_ROWS, t_), 1)
    qchunk = (qcol // PERM_BLOCK * PERM_BLOCK + _perm_token(qcol)) // CHUNK
    mask_rows = jnp.where((crow < t_ // CHUNK) & (crow > qchunk), NEG_BIG, 0.0).astype(BF16)
    for hh in range(nh):
        r0 = hh * hp + MLA_QK
        for qq in range(ATT_NQ):
            q2_ref[1, r0:r0 + ATT_MASK_ROWS, qq * t_:(qq + 1) * t_] = mask_rows
    p_ref[...] = jnp.zeros(p_ref.shape, BF16)
    acc_ref[...] = jnp.zeros(acc_ref.shape, F32)

    def nxt(qi, j):
        last = j >= qi
        return jnp.where(last, qi + 1, qi), jnp.where(last, 0, j + 1)

    def qk(slot, qi, j):
        diag = (j == qi).astype(jnp.int32)
        k0 = pl.multiple_of(j * t_, t_)
        q0 = pl.multiple_of(qi * t_, t_)
        for hh in range(nh):
            kt = k_ref[pl.ds(k0, t_), hh * hp:(hh + 1) * hp]
            qv = q2_ref[diag, hh * hp:(hh + 1) * hp, pl.ds(q0, t_)]
            st = _dot(kt, qv)
            s_ref[slot, hh] = st
            mt_ref[slot, hh] = jnp.max(st, axis=0, keepdims=True)

    def sm(slot, j, m):
        fresh = j == 0
        m_out, a_out = [], []
        for hh in range(nh):
            m_old = jnp.where(fresh, NEG_BIG, m[hh])
            m_new = jnp.maximum(m_old, mt_ref[slot, hh])
            m_out.append(m_new)
            a_out.append(jnp.exp2(m_old - m_new))
            p_ref[slot, hh] = jnp.exp2(s_ref[slot, hh] - m_new).astype(BF16)
        return tuple(m_out), tuple(a_out)

    def pv(slot, qi, j, alpha):
        k0 = pl.multiple_of(j * t_, t_)
        q0 = pl.multiple_of(qi * t_, t_)
        for hh in range(nh):
            vt = vt_ref[0, hh * ATT_V_PAD:(hh + 1) * ATT_V_PAD, pl.ds(k0, t_)]
            upd = _dot(vt, p_ref[slot, hh])
            acc_old = jnp.where(j == 0, 0.0, acc_ref[hh])
            acc_new = acc_old * alpha[hh] + upd
            acc_ref[hh] = acc_new
            o_ref[0, hh * MLA_V:(hh + 1) * MLA_V, pl.ds(q0, t_)] = (
                acc_new[0:MLA_V, :] * (1.0 / acc_new[MLA_V:MLA_V + 1, :])).astype(BF16)

    def step(par, carry, do_qk=True, do_sm=True):
        qa, ja, qb, jb, qc, jc, m, al = carry
        pv(par, qc, jc, al)
        if do_sm:
            m2, al2 = sm(1 - par, jb, m)
        else:
            m2, al2 = m, al
        if do_qk:
            qk(par, qa, ja)
        qa2, ja2 = nxt(qa, ja)
        return (qa2, ja2, qa, ja, qb, jb, m2, al2)

    zero = jnp.int32(0)
    qk(0, zero, zero)
    row = lambda v: tuple(jnp.full((1, t_), v, F32) for _ in range(nh))
    carry = (jnp.int32(1), zero, zero, zero, zero, zero, row(NEG_BIG), row(1.0))
    n_items = ATT_NQ * (ATT_NQ + 1) // 2
    assert n_items % 2 == 0
    quads, rest = divmod(n_items - 2, 4)
    assert rest == 2
    carry = lax.fori_loop(0, quads, lambda u, cr: step(0, step(1, step(0, step(1, cr)))), carry)
    carry = step(0, step(1, carry))
    carry = step(1, carry)
    carry = step(0, carry, do_qk=False)
    step(1, carry, do_qk=False, do_sm=False)


def _attention(qt, k, vt):
    nh = ATT_HEADS_PER_STEP
    return pl.pallas_call(
        _attn_kernel,
        grid=(BATCH, MLA_HEADS // nh),
        in_specs=[
            pl.BlockSpec((1, nh * MLA_HEAD_PAD, SEQ), lambda b, h: (b, h, 0)),
            pl.BlockSpec((SEQ, nh * MLA_HEAD_PAD), lambda b, h: (b, h)),
            pl.BlockSpec((1, nh * ATT_V_PAD, SEQ), lambda b, h: (b, h, 0)),
        ],
        out_specs=pl.BlockSpec((1, nh * MLA_V, SEQ), lambda b, h: (b, h, 0)),
        out_shape=jax.ShapeDtypeStruct((BATCH, MLA_HEADS * MLA_V, SEQ), BF16),
        scratch_shapes=[
            pltpu.VMEM((2, nh * MLA_HEAD_PAD, SEQ), BF16),
            pltpu.VMEM((2, nh, ATT_T, ATT_T), F32),
            pltpu.VMEM((2, nh, ATT_T, ATT_T), BF16),
            pltpu.VMEM((nh, ATT_V_PAD, ATT_T), F32),
            pltpu.VMEM((2, nh, 1, ATT_T), F32),
        ],
        compiler_params=_params(("arbitrary", "arbitrary")),
        name="mla_attention",
    )(qt, k, vt)


PROJ_TM = 512


def _proj_ln_kernel(x_ref, at_ref, w_ref, g_ref, b_ref, o_ref):
    y = lax.dot_general(at_ref[0], w_ref[...], (((0,), (0,)), ((), ())), preferred_element_type=F32)
    z = DN_ALPHA * x_ref[...] + y
    o_ref[...] = _layer_norm(z, g_ref[...], b_ref[...])


def _proj_ln(x, a_t, w, g, b):
    tm = PROJ_TM
    nl = SEQ // tm
    kdim = a_t.shape[1]
    return pl.pallas_call(
        _proj_ln_kernel,
        grid=(BATCH, nl),
        in_specs=[
            pl.BlockSpec((tm, D_MODEL), lambda bi, j: (bi * nl + j, 0)),
            pl.BlockSpec((1, kdim, tm), lambda bi, j: (bi, 0, j)),
            _const_spec((kdim, D_MODEL)),
            _const_spec((1, D_MODEL)),
            _const_spec((1, D_MODEL)),
        ],
        out_specs=pl.BlockSpec((tm, D_MODEL), lambda bi, j: (bi * nl + j, 0)),
        out_shape=jax.ShapeDtypeStruct((TOKENS, D_MODEL), F32),
        compiler_params=_params(("arbitrary", "arbitrary")),
        name="proj_ln",
    )(x, a_t, w, g, b)


SSM_TM = 512
SSD_Q = 128
SSD_TM = 512
LOG2E = math.log2(math.e)


def _ssm_in_kernel(x_ref, wz_ref, wxbc_ref, wdt_ref, cw_ref, cb_ref, dtb_ref,
                   z_ref, xs_ref, b_ref, bt_ref, c_ref, dt_ref, dtt_ref, buf_ref, halo_ref):
    tm = SSM_TM
    blk = CONV_BLK
    nblk = SSM_CONV_DIM // blk
    nz = SSM_D_INNER // blk

    @pl.when(pl.program_id(1) == 0)
    def _():
        halo_ref[...] = jnp.zeros(halo_ref.shape, F32)

    xb = x_ref[...].astype(BF16)
    dt = jax.nn.softplus(_dot(xb, wdt_ref[...]) + dtb_ref[...])
    dt_ref[...] = dt
    dtt_ref[0] = dt.T[0:SSM_HEADS, :]

    def project(k):
        _conv_store(buf_ref.at[k % 4], _dot(xb, wxbc_ref[:, k * blk:(k + 1) * blk]), SSM_CONV, tm)

    def gate_proj(k):
        z_ref[:, k * blk:(k + 1) * blk] = _dot(xb, wz_ref[:, k * blk:(k + 1) * blk]).astype(BF16)

    def conv(k):
        c0 = k * blk
        v = _silu(jnp.concatenate(_conv_block(buf_ref.at[k % 4], halo_ref, cw_ref, cb_ref, c0, SSM_CONV, tm), axis=0))
        if c0 < SSM_D_INNER:
            xs_ref[:, c0:c0 + blk] = v.astype(BF16)
        elif c0 < SSM_D_INNER + SSM_BC:
            n0 = c0 - SSM_D_INNER
            b_ref[:, n0:n0 + blk] = v.astype(BF16)
            bt_ref[0, n0:n0 + blk, :] = v.T.astype(BF16)
        else:
            n0 = c0 - SSM_D_INNER - SSM_BC
            c_ref[:, n0:n0 + blk] = v.astype(BF16)

    assert nblk == 2 * nz
    project(0)
    project(1)
    for k in range(0, nblk, 2):
        if k + 2 < nblk:
            project(k + 2)
            project(k + 3)
        gate_proj(k // 2)
        conv(k)
        conv(k + 1)


def _ssm_in(x, w_z, w_xbc, w_dt, conv_w, conv_b, dt_bias):
    tm = SSM_TM
    nl = SEQ // tm
    tok = lambda n: pl.BlockSpec((tm, n), lambda b, j: (b * nl + j, 0))
    return pl.pallas_call(
        _ssm_in_kernel,
        grid=(BATCH, nl),
        in_specs=[
            tok(D_MODEL),
            _const_spec((D_MODEL, SSM_D_INNER)),
            _const_spec((D_MODEL, SSM_CONV_DIM)),
            _const_spec((D_MODEL, LANES)),
            _const_spec((SSM_CONV, SSM_CONV_DIM)),
            _const_spec((1, SSM_CONV_DIM)),
            _const_spec((1, LANES)),
        ],
        out_specs=[
            tok(SSM_D_INNER),
            tok(SSM_D_INNER),
            tok(SSM_BC),
            pl.BlockSpec((1, SSM_BC, tm), lambda b, j: (b, 0, j)),
            tok(SSM_BC),
            tok(LANES),
            pl.BlockSpec((1, SSM_HEADS, tm), lambda b, j: (b, 0, j)),
        ],
        out_shape=[
            jax.ShapeDtypeStruct((TOKENS, SSM_D_INNER), BF16),
            jax.ShapeDtypeStruct((TOKENS, SSM_D_INNER), BF16),
            jax.ShapeDtypeStruct((TOKENS, SSM_BC), BF16),
            jax.ShapeDtypeStruct((BATCH, SSM_BC, SEQ), BF16),
            jax.ShapeDtypeStruct((TOKENS, SSM_BC), BF16),
            jax.ShapeDtypeStruct((TOKENS, LANES), F32),
            jax.ShapeDtypeStruct((BATCH, SSM_HEADS, SEQ), F32),
        ],
        scratch_shapes=[
            pltpu.VMEM((4, _conv_rows(SSM_CONV, tm), CONV_BLK), F32),
            pltpu.VMEM(((SSM_CONV - 1) * SUBLANES, SSM_CONV_DIM), F32),
        ],
        compiler_params=_params(("arbitrary", "arbitrary")),
        name="ssm_in",
    )(x, w_z, w_xbc, w_dt, conv_w, conv_b, dt_bias)


def _ssd_kernel(x_ref, z_ref, xs_ref, b_ref, bt_ref, c_ref, dt_ref, dtt_ref,
                alog_row_ref, alog_col_ref, dskip_ref, ng_ref, rexp_ref, wout_ref, g_ref, bb_ref,
                o_ref, state_ref, y_ref, xbd_ref, yb_ref):
    q = SSD_Q
    hi = lax.Precision.HIGHEST
    hw = SSM_HPG * SSM_HEAD_DIM

    @pl.when(pl.program_id(1) == 0)
    def _():
        state_ref[...] = jnp.zeros(state_ref.shape, F32)
        xbd_ref[...] = jnp.zeros(xbd_ref.shape, BF16)

    row = lax.broadcasted_iota(jnp.int32, (q, q), 0)
    col = lax.broadcasted_iota(jnp.int32, (q, q), 1)
    lower = _perm_token(col) <= _perm_token(row)
    tri = lower.astype(F32)
    a_row = -jnp.exp(alog_row_ref[...]) * LOG2E
    a_col = -jnp.exp(alog_col_ref[...]) * LOG2E
    lane_head = lax.broadcasted_iota(jnp.int32, (1, hw), 1) // SSM_HEAD_DIM

    def scan(c):
        rows = slice(c * q, (c + 1) * q)
        dt = dt_ref[rows, :]
        dt_t = dtt_ref[0, :, rows]
        acum = jnp.dot(tri, dt * a_row, precision=hi, preferred_element_type=F32)
        acum_t = lax.dot_general(dt_t * a_col, tri, (((1,), (1,)), ((), ())), precision=hi,
                                 preferred_element_type=F32)
        e_col = jnp.exp2(acum).astype(BF16)
        w_col = (jnp.exp2(acum[q - 1:q, :] - acum) * dt).astype(BF16)
        rowterm = acum_t - jnp.log2(dt_t)
        cd = jnp.exp2(acum_t[:, q - 1:q])
        for grp in range(SSM_GROUPS):
            n0 = grp * SSM_STATE
            c0 = grp * hw
            cg = c_ref[rows, n0:n0 + SSM_STATE]
            cb = _dot_nt(cg, b_ref[rows, n0:n0 + SSM_STATE])
            rg = rexp_ref[:, c0:c0 + hw]
            eg = _dot(e_col, rg)
            wg = _dot(w_col, rg)
            ms = []
            cdg = jnp.zeros((1, hw), F32)
            for r in range(SSM_HPG):
                h = grp * SSM_HPG + r
                p0 = c0 + r * SSM_HEAD_DIM
                seg = acum[:, h:h + 1] - rowterm[h:h + 1, :]
                ms.append((cb * jnp.exp2(jnp.where(lower, seg, NEG_BIG))).astype(BF16))
                xbd_ref[c, grp, r * q:(r + 1) * q, r * SSM_HEAD_DIM:(r + 1) * SSM_HEAD_DIM] = xs_ref[rows, p0:p0 + SSM_HEAD_DIM]
                cdg = jnp.where(lane_head == r, cd[h:h + 1, :], cdg)
            s_old = state_ref[grp]
            xgf = xs_ref[rows, c0:c0 + hw].astype(F32)
            yg = (_dot(jnp.concatenate(ms, axis=1), xbd_ref[c, grp])
                  + _dot(cg, s_old.astype(BF16)) * eg + dskip_ref[:, c0:c0 + hw] * xgf)
            y_ref[rows, c0:c0 + hw] = yg
            state_ref[grp] = s_old * cdg + _dot(bt_ref[0, n0:n0 + SSM_STATE, rows], (xgf * wg).astype(BF16))

    def finish(c):
        rows = slice(c * q, (c + 1) * q)
        y = y_ref[rows, :] * _silu(z_ref[rows, :].astype(F32))
        for grp in range(SSM_GROUPS):
            c0 = grp * hw
            yb_ref[rows, c0:c0 + hw] = _rms(y[:, c0:c0 + hw], ng_ref[:, c0:c0 + hw]).astype(BF16)
        zres = DN_ALPHA * x_ref[rows, :] + _dot(yb_ref[rows, :], wout_ref[...])
        o_ref[rows, :] = _layer_norm(zres, g_ref[...], bb_ref[...])

    nc = SSD_TM // q
    for c in range(nc + 1):
        if c < nc:
            scan(c)
        if c >= 1:
            finish(c - 1)


def _ssd(x, z, xs, bm, bt, cm, dt, dtt, alog_row, alog_col, dskip, ng, rexp, w_out, g, b):
    tm = SSD_TM
    nl = SEQ // tm
    tok = lambda n: pl.BlockSpec((tm, n), lambda bi, j: (bi * nl + j, 0))
    return pl.pallas_call(
        _ssd_kernel,
        grid=(BATCH, nl),
        in_specs=[
            tok(D_MODEL),
            tok(SSM_D_INNER),
            tok(SSM_D_INNER),
            tok(SSM_BC),
            pl.BlockSpec((1, SSM_BC, tm), lambda bi, j: (bi, 0, j)),
            tok(SSM_BC),
            tok(LANES),
            pl.BlockSpec((1, SSM_HEADS, tm), lambda bi, j: (bi, 0, j)),
            _const_spec((1, LANES)),
            _const_spec((SSM_HEADS, 1)),
            _const_spec((1, SSM_D_INNER)),
            _const_spec((1, SSM_D_INNER)),
            _const_spec((LANES, SSM_D_INNER)),
            _const_spec((SSM_D_INNER, D_MODEL)),
            _const_spec((1, D_MODEL)),
            _const_spec((1, D_MODEL)),
        ],
        out_specs=tok(D_MODEL),
        out_shape=jax.ShapeDtypeStruct((TOKENS, D_MODEL), F32),
        scratch_shapes=[
            pltpu.VMEM((SSM_GROUPS, SSM_STATE, SSM_HPG * SSM_HEAD_DIM), F32),
            pltpu.VMEM((tm, SSM_D_INNER), F32),
            pltpu.VMEM((tm // SSD_Q, SSM_GROUPS, SSM_HPG * SSD_Q, SSM_HPG * SSM_HEAD_DIM), BF16),
            pltpu.VMEM((tm, SSM_D_INNER), BF16),
        ],
        compiler_params=_params(("arbitrary", "arbitrary")),
        name="ssd_scan",
    )(x, z, xs, bm, bt, cm, dt, dtt, alog_row, alog_col, dskip, ng, rexp, w_out, g, b)


def _row(v):
    return v.reshape(1, -1).astype(F32)


def _mamba_layer(x, w_in, conv_w, conv_b, dt_bias, a_log, d_skip, norm_g, w_out, g, b):
    w_z = w_in[:, :SSM_D_INNER].astype(BF16)
    w_xbc = w_in[:, SSM_D_INNER:SSM_D_INNER + SSM_CONV_DIM].astype(BF16)
    w_dt = jnp.pad(w_in[:, SSM_D_INNER + SSM_CONV_DIM:], ((0, 0), (0, LANES - SSM_HEADS))).astype(BF16)
    dtb = jnp.pad(dt_bias, (0, LANES - SSM_HEADS)).reshape(1, LANES)
    z, xs, bm, bt, cm, dt, dtt = _ssm_in(x, w_z, w_xbc, w_dt, conv_w, _row(conv_b), dtb)
    alog_row = jnp.pad(a_log, (0, LANES - SSM_HEADS)).reshape(1, LANES)
    alog_col = a_log.reshape(SSM_HEADS, 1)
    dskip = jnp.repeat(d_skip, SSM_HEAD_DIM).reshape(1, SSM_D_INNER)
    rexp = jnp.repeat(jnp.eye(LANES, SSM_HEADS, dtype=F32), SSM_HEAD_DIM, axis=1).astype(BF16)
    return _ssd(x, z, xs, bm, bt, cm, dt, dtt, alog_row, alog_col, dskip, _row(norm_g), rexp,
                w_out.astype(BF16), _row(g), _row(b))


def _sg_layer(x, w_in, b_in, ln_g, ln_b, w_s, b_s, w_out, g, b):
    tok = _perm_token(jnp.arange(SG_BLOCK))
    b_s_wide = jnp.repeat(b_s.T[tok], SG_GROUP_DIM, axis=1)
    w_s = w_s[:, tok][:, :, tok]
    return _spatial_gating(x, w_in.astype(BF16), _row(b_in), _row(ln_g), _row(ln_b), w_s, b_s_wide,
                           w_out.astype(BF16), _row(g), _row(b))


def _mla_layer(x, positions, w_in, q_norm_g, w_q_b, kv_norm_g, w_kv_b, w_out, g, b):
    h = MLA_HEADS
    s0 = MLA_Q_RANK + MLA_KV_RANK
    t1 = w_in[:, s0:s0 + MLA_HALF]
    t2 = w_in[:, s0 + MLA_HALF:s0 + MLA_ROPE]
    zl = jnp.zeros((D_MODEL, MLA_NOPE), F32)
    zh = jnp.zeros((D_MODEL, LANES - MLA_QK), F32)
    w_in_pad = jnp.concatenate([w_in[:, :s0], zl, t1, t2, zh, zl, -t2, t1, zh], axis=1).astype(BF16)

    wq = w_q_b.reshape(MLA_Q_RANK, h, MLA_QK)
    wq = jnp.pad(wq, ((0, 0), (0, 0), (0, MLA_HEAD_PAD - MLA_QK)))
    wqt = wq.reshape(MLA_Q_RANK, h * MLA_HEAD_PAD).T.astype(BF16)
    wkv = w_kv_b.reshape(MLA_KV_RANK, h, MLA_NOPE + MLA_V)
    wk = jnp.pad(wkv[:, :, :MLA_NOPE], ((0, 0), (0, 0), (0, MLA_HEAD_PAD - MLA_NOPE)))
    wk = wk.reshape(MLA_KV_RANK, h * MLA_HEAD_PAD).astype(BF16)
    wvt = wkv[:, :, MLA_NOPE:].reshape(MLA_KV_RANK, h * MLA_V).T.astype(BF16)

    inv = (ROPE_THETA ** (-(jnp.arange(MLA_HALF, dtype=F32) * 2.0 / MLA_ROPE))).reshape(MLA_HALF, 1)
    cos_t, sin_t, cos_tab, sin_tab = _rope_tables(positions, inv)
    qt, k, vt = _mla_proj(x, w_in_pad, _row(q_norm_g), _row(kv_norm_g), wqt, wk, wvt,
                          cos_t, sin_t, cos_tab, sin_tab)
    o = _attention(qt, k, vt)
    return _proj_ln(x, o, w_out.astype(BF16), _row(g), _row(b))


def kernel(x, positions, ssm_w_in, ssm_conv_w, ssm_conv_b, ssm_dt_bias, ssm_a_log, ssm_d, ssm_norm_g, ssm_w_out, sg_w_in, sg_b_in, sg_ln_g, sg_ln_b, sg_w_s, sg_b_s, sg_w_out, mla_w_in, mla_q_norm_g, mla_w_q_b, mla_kv_norm_g, mla_w_kv_b, mla_w_out, ffn_w_in, ffn_conv_w, ffn_conv_b, ffn_w_out, ln_g, ln_b):
    h = _permute_tokens(x).reshape(TOKENS, D_MODEL)
    positions = _permute_tokens(positions)
    for i in range(DEPTH):
        m, j = i % N_MIXERS, i // N_MIXERS
        g0, b0 = ln_g[i, 0], ln_b[i, 0]
        if m == 0:
            h = _mamba_layer(h, ssm_w_in[j], ssm_conv_w[j], ssm_conv_b[j], ssm_dt_bias[j], ssm_a_log[j],
                             ssm_d[j], ssm_norm_g[j], ssm_w_out[j], g0, b0)
        elif m == 1:
            h = _sg_layer(h, sg_w_in[j], sg_b_in[j], sg_ln_g[j], sg_ln_b[j], sg_w_s[j], sg_b_s[j],
                          sg_w_out[j], g0, b0)
        else:
            h = _mla_layer(h, positions, mla_w_in[j], mla_q_norm_g[j], mla_w_q_b[j], mla_kv_norm_g[j],
                           mla_w_kv_b[j], mla_w_out[j], g0, b0)
        h = _ffn(h, ffn_w_in[i].astype(BF16), ffn_conv_w[i], _row(ffn_conv_b[i]),
                 ffn_w_out[i].astype(BF16), _row(ln_g[i, 1]), _row(ln_b[i, 1]))
    return _unpermute_tokens(h.reshape(BATCH, SEQ, D_MODEL))
```
